```python
import jax, jax.numpy as jnp
from jax import lax
import numpy as np

D_MODEL = 2048
BATCH = 4
SEQ = 4096
DEPTH = 1
DEC_BATCH = 32
DEC_SEQ = 32
PAST_LEN = 1024

CHUNK = 64
Q_BLOCK = 128
EPS = 1e-6
NEG_INF = -1e30
N_MOD = 6
MLA_HEADS = 8
QK_NOPE = 128
QK_ROPE = 64
V_HEAD = 128
KV_LORA = 512
ROPE_THETA = 10000.0
FOX_HEADS = 8
FOX_HEAD_DIM = 128
FORGET_BIAS_INIT = 2.0
PEER_HEADS = 8
N_KEYS = 128
N_EXPERTS = N_KEYS * N_KEYS
PEER_QDIM = 256
PEER_TOPK = 16
PEER_BLOCK = 128

MLA_Q_COLS = MLA_HEADS * (QK_NOPE + QK_ROPE)
FOX_COLS = FOX_HEADS * FOX_HEAD_DIM
SPLIT_0 = MLA_Q_COLS
SPLIT_1 = SPLIT_0 + KV_LORA
SPLIT_2 = SPLIT_1 + QK_ROPE
SPLIT_3 = SPLIT_2 + 3 * FOX_COLS
SPLIT_4 = SPLIT_3 + FOX_HEADS
SPLIT_5 = SPLIT_4 + D_MODEL
IN_COLS = SPLIT_5 + D_MODEL

kernel_name = "mla_fox_peer_streaming_encoder_step"


def rms_norm(x, g):
    x32 = x.astype(jnp.float32)
    y = x32 * lax.rsqrt(jnp.mean(x32 * x32, axis=-1, keepdims=True) + EPS)
    return (y * g.astype(jnp.float32)).astype(x.dtype)


def modulate(x, g, shift, scale):
    return rms_norm(x, g) * (1 + scale[:, None, :]) + shift[:, None, :]


def rope(x, pos):
    half = QK_ROPE // 2
    inv = 1.0 / (ROPE_THETA ** (jnp.arange(0, QK_ROPE, 2, dtype=jnp.float32) / QK_ROPE))
    ang = pos.astype(jnp.float32)[:, None] * inv[None, :]
    cos = jnp.cos(ang)[None, :, None, :]
    sin = jnp.sin(ang)[None, :, None, :]
    x32 = x.astype(jnp.float32)
    x1, x2 = x32[..., :half], x32[..., half:]
    return jnp.concatenate([x1 * cos - x2 * sin, x1 * sin + x2 * cos], axis=-1).astype(x.dtype)


def block_attention(q, k, v, q_pos, k_pos, chunk_causal, f_q=None, f_k=None):
    b, sq, h, dk = q.shape
    scale = dk ** -0.5

    def attend(args):
        qb, pb, fb = args
        s = jnp.einsum('bqhd,bkhd->bhqk', qb, k).astype(jnp.float32) * scale
        if f_k is not None:
            s = s + (jnp.swapaxes(fb, 1, 2)[..., :, None] - jnp.swapaxes(f_k, 1, 2)[..., None, :])
        if chunk_causal:
            ok = (pb[:, None] // CHUNK) >= (k_pos[None, :] // CHUNK)
        else:
            ok = pb[:, None] >= k_pos[None, :]
        s = jnp.where(ok, s, NEG_INF)
        p = jax.nn.softmax(s, axis=-1).astype(v.dtype)
        return jnp.einsum('bhqk,bkhd->bqhd', p, v)

    if sq > Q_BLOCK and sq % Q_BLOCK == 0:
        nb = sq // Q_BLOCK
        to_blocks = lambda a: jnp.moveaxis(a.reshape((b, nb, Q_BLOCK) + a.shape[2:]), 1, 0)
        fb = None if f_q is None else to_blocks(f_q)
        out = lax.map(attend, (to_blocks(q), q_pos.reshape(nb, Q_BLOCK), fb))
        return jnp.moveaxis(out, 0, 1).reshape(b, sq, h, -1)
    return attend((q, q_pos, f_q))


def mla_mixer(q_cols, ckv, kr, pos, cache_lat, cache_kr, kv_norm_g, w_ukv):
    b, s, _ = q_cols.shape
    q = q_cols.reshape(b, s, MLA_HEADS, QK_NOPE + QK_ROPE)
    q = jnp.concatenate([q[..., :QK_NOPE], rope(q[..., QK_NOPE:], pos)], axis=-1)
    lat = rms_norm(ckv, kv_norm_g)
    k_r = rope(kr[:, :, None, :], pos)[:, :, 0, :]
    lat_all = jnp.concatenate([cache_lat, lat], axis=1)
    kr_all = jnp.concatenate([cache_kr, k_r], axis=1)
    n_keys = lat_all.shape[1]
    kv = jnp.einsum('blc,ce->ble', lat_all, w_ukv).reshape(b, n_keys, MLA_HEADS, QK_NOPE + V_HEAD)
    k_nope, v = kv[..., :QK_NOPE], kv[..., QK_NOPE:]
    k = jnp.concatenate([k_nope, jnp.broadcast_to(kr_all[:, :, None, :], (b, n_keys, MLA_HEADS, QK_ROPE))], axis=-1)
    k_pos = jnp.arange(n_keys, dtype=jnp.int32)
    o = block_attention(q, k, v, pos, k_pos, True)
    return o.reshape(b, s, MLA_HEADS * V_HEAD), lat, k_r


def fox_mixer(qkv, f_logit, f_bias, pos, cache_k, cache_v, cache_logf):
    b, s, _ = qkv.shape
    past = cache_k.shape[1]
    qkv = qkv.reshape(b, s, 3, FOX_HEADS, FOX_HEAD_DIM)
    q, k, v = qkv[:, :, 0], qkv[:, :, 1], qkv[:, :, 2]
    logf = jax.nn.log_sigmoid(f_logit.astype(jnp.float32) + f_bias.astype(jnp.float32))
    k_all = jnp.concatenate([cache_k, k], axis=1)
    v_all = jnp.concatenate([cache_v, v], axis=1)
    f_all = jnp.cumsum(jnp.concatenate([cache_logf.astype(jnp.float32), logf], axis=1), axis=1)
    k_pos = jnp.arange(past + s, dtype=jnp.int32)
    o = block_attention(q, k_all, v_all, pos, k_pos, False, f_all[:, past:], f_all)
    return o.reshape(b, s, FOX_COLS), k, v, logf


def peer_ffn(h, w_q, keys1, keys2, u_tab, v_tab):
    b, s, d = h.shape
    t = b * s
    hf = h.reshape(t, d)
    q = (hf @ w_q).reshape(t, PEER_HEADS, PEER_QDIM).astype(jnp.float32)
    half = PEER_QDIM // 2
    s1 = jnp.einsum('thd,nd->thn', q[..., :half], keys1.astype(jnp.float32))
    s2 = jnp.einsum('thd,nd->thn', q[..., half:], keys2.astype(jnp.float32))
    v1, i1 = lax.top_k(s1, PEER_TOPK)
    v2, i2 = lax.top_k(s2, PEER_TOPK)
    cand = (v1[..., :, None] + v2[..., None, :]).reshape(t, PEER_HEADS, PEER_TOPK * PEER_TOPK)
    cidx = (i1[..., :, None] * N_KEYS + i2[..., None, :]).reshape(t, PEER_HEADS, PEER_TOPK * PEER_TOPK)
    sc, sel = lax.top_k(cand, PEER_TOPK)
    expert = jnp.take_along_axis(cidx, sel, axis=-1)
    gate = jax.nn.softmax(sc, axis=-1)
    pad = (-t) % PEER_BLOCK
    nb = (t + pad) // PEER_BLOCK
    hb = jnp.pad(hf, ((0, pad), (0, 0))).reshape(nb, PEER_BLOCK, d)
    eb = jnp.pad(expert, ((0, pad), (0, 0), (0, 0))).reshape(nb, PEER_BLOCK, PEER_HEADS, PEER_TOPK)
    gb = jnp.pad(gate, ((0, pad), (0, 0), (0, 0))).reshape(nb, PEER_BLOCK, PEER_HEADS, PEER_TOPK)

    def expert_block(args):
        hx, ex, gx = args
        a = jnp.einsum('td,thkd->thk', hx, u_tab[ex]).astype(jnp.float32)
        w = (gx * jax.nn.gelu(a, approximate=False)).astype(hx.dtype)
        return jnp.einsum('thk,thkd->td', w, v_tab[ex])

    out = lax.map(expert_block, (hb, eb, gb)).reshape(nb * PEER_BLOCK, d)[:t]
    return out.reshape(b, s, d)


def trunk_layer(x, c, cache_lat, cache_kr, cache_k, cache_v, cache_logf,
                w_ada, b_ada, norm1_g, w_in, fox_fbias, kv_norm_g, w_ukv, w_o_mla, w_o_fox,
                w_out, norm2_g, peer_wq, peer_keys1, peer_keys2, peer_u, peer_v):
    b, s, _ = x.shape
    past = cache_lat.shape[1]
    pos = past + jnp.arange(s, dtype=jnp.int32)
    mod = jax.nn.silu(c) @ w_ada + b_ada
    shift1, scale1, gate1, shift2, scale2, gate2 = jnp.split(mod, N_MOD, axis=-1)
    h = modulate(x, norm1_g, shift1, scale1)
    proj = h @ w_in
    q_mla, ckv, kr, qkv_fox, f_logit, g_mla, g_fox = jnp.split(
        proj, [SPLIT_0, SPLIT_1, SPLIT_2, SPLIT_3, SPLIT_4, SPLIT_5], axis=-1)
    o_mla, lat_new, kr_new = mla_mixer(q_mla, ckv, kr, pos, cache_lat, cache_kr, kv_norm_g, w_ukv)
    o_fox, k_new, v_new, logf_new = fox_mixer(qkv_fox, f_logit, fox_fbias, pos, cache_k, cache_v, cache_logf)
    merged = jax.nn.sigmoid(g_mla) * (o_mla @ w_o_mla) + jax.nn.sigmoid(g_fox) * (o_fox @ w_o_fox)
    x = x + gate1[:, None, :] * (merged @ w_out)
    h2 = modulate(x, norm2_g, shift2, scale2)
    x = x + gate2[:, None, :] * peer_ffn(h2, peer_wq, peer_keys1, peer_keys2, peer_u, peer_v)
    return x, (lat_new, kr_new, k_new, v_new, logf_new)


def setup_inputs(seed: int = 0) -> dict:
    key = jax.random.key(seed)
    ks = jax.random.split(key, 32)
    nrm = lambda k, shape, sc: jax.random.normal(k, shape, jnp.float32) * sc
    d = D_MODEL
    return {
        'x_prompt': nrm(ks[0], (BATCH, SEQ, d), 1.0),
        'x_sample': nrm(ks[1], (DEC_BATCH, DEC_SEQ, d), 1.0),
        'c_prompt': nrm(ks[2], (BATCH, d), 1.0),
        'c_sample': nrm(ks[3], (DEC_BATCH, d), 1.0),
        'cache_mla_latent': nrm(ks[4], (DEPTH, DEC_BATCH, PAST_LEN, KV_LORA), 1.0),
        'cache_mla_krope': nrm(ks[5], (DEPTH, DEC_BATCH, PAST_LEN, QK_ROPE), 1.0),
        'cache_fox_k': nrm(ks[6], (DEPTH, DEC_BATCH, PAST_LEN, FOX_HEADS, FOX_HEAD_DIM), 1.0),
        'cache_fox_v': nrm(ks[7], (DEPTH, DEC_BATCH, PAST_LEN, FOX_HEADS, FOX_HEAD_DIM), 1.0),
        'cache_fox_logf': jax.nn.log_sigmoid(nrm(ks[8], (DEPTH, DEC_BATCH, PAST_LEN, FOX_HEADS), 1.0) + FORGET_BIAS_INIT),
        'w_ada': nrm(ks[9], (DEPTH, d, N_MOD * d), 0.5 * d ** -0.5),
        'b_ada': nrm(ks[10], (DEPTH, N_MOD * d), 0.02),
        'norm1_g': 1.0 + nrm(ks[11], (DEPTH, d), 0.02),
        'w_in': nrm(ks[12], (DEPTH, d, IN_COLS), d ** -0.5),
        'fox_fbias': FORGET_BIAS_INIT + nrm(ks[13], (DEPTH, FOX_HEADS), 0.5),
        'kv_norm_g': 1.0 + nrm(ks[14], (DEPTH, KV_LORA), 0.02),
        'w_ukv': nrm(ks[15], (DEPTH, KV_LORA, MLA_HEADS * (QK_NOPE + V_HEAD)), KV_LORA ** -0.5),
        'w_o_mla': nrm(ks[16], (DEPTH, MLA_HEADS * V_HEAD, d), (MLA_HEADS * V_HEAD) ** -0.5),
        'w_o_fox': nrm(ks[17], (DEPTH, FOX_COLS, d), FOX_COLS ** -0.5),
        'w_out': nrm(ks[18], (DEPTH, d, d), d ** -0.5),
        'norm2_g': 1.0 + nrm(ks[19], (DEPTH, d), 0.02),
        'peer_wq': nrm(ks[20], (DEPTH, d, PEER_HEADS * PEER_QDIM), d ** -0.5),
        'peer_keys1': nrm(ks[21], (DEPTH, N_KEYS, PEER_QDIM // 2), (PEER_QDIM // 2) ** -0.5),
        'peer_keys2': nrm(ks[22], (DEPTH, N_KEYS, PEER_QDIM // 2), (PEER_QDIM // 2) ** -0.5),
        'peer_u': nrm(ks[23], (DEPTH, N_EXPERTS, d), d ** -0.5),
        'peer_v': nrm(ks[24], (DEPTH, N_EXPERTS, d), 1.0),
        'final_g': 1.0 + nrm(ks[25], (d,), 0.02),
    }


def reference(x_prompt, x_sample, c_prompt, c_sample, cache_mla_latent, cache_mla_krope,
              cache_fox_k, cache_fox_v, cache_fox_logf, w_ada, b_ada, norm1_g, w_in, fox_fbias,
              kv_norm_g, w_ukv, w_o_mla, w_o_fox, w_out, norm2_g, peer_wq, peer_keys1, peer_keys2,
              peer_u, peer_v, final_g):
    xp, xs = x_prompt, x_sample
    bp = xp.shape[0]
    states_p, states_s = [], []
    for l in range(DEPTH):
        w = (w_ada[l], b_ada[l], norm1_g[l], w_in[l], fox_fbias[l], kv_norm_g[l], w_ukv[l],
             w_o_mla[l], w_o_fox[l], w_out[l], norm2_g[l], peer_wq[l], peer_keys1[l],
             peer_keys2[l], peer_u[l], peer_v[l])
        xp, st_p = trunk_layer(xp, c_prompt,
                               jnp.zeros((bp, 0, KV_LORA), xp.dtype),
                               jnp.zeros((bp, 0, QK_ROPE), xp.dtype),
                               jnp.zeros((bp, 0, FOX_HEADS, FOX_HEAD_DIM), xp.dtype),
                               jnp.zeros((bp, 0, FOX_HEADS, FOX_HEAD_DIM), xp.dtype),
                               jnp.zeros((bp, 0, FOX_HEADS), jnp.float32), *w)
        xs, st_s = trunk_layer(xs, c_sample, cache_mla_latent[l], cache_mla_krope[l],
                               cache_fox_k[l], cache_fox_v[l], cache_fox_logf[l], *w)
        states_p.append(st_p)
        states_s.append(st_s)
    sp = [jnp.stack([st[i] for st in states_p]) for i in range(5)]
    ss = [jnp.stack([st[i] for st in states_s]) for i in range(5)]
    y_prompt = rms_norm(xp, final_g)
    y_sample = rms_norm(xs, final_g)
    return (y_prompt, y_sample, sp[0], sp[1], sp[2], sp[3], sp[4], ss[0], ss[1], ss[2], ss[3], ss[4])
```

```python
import functools
import math

import jax
import jax.numpy as jnp
from jax import lax
from jax.experimental import pallas as pl
from jax.experimental.pallas import tpu as pltpu

F32 = jnp.float32
BF16 = jnp.bfloat16

D_MODEL = 2048
EPS = 1e-6
NEG_INF = -1e30
N_MOD = 6
HEADS = 8
HEAD_DIM = 128
QK_ROPE = 64
KV_LORA = 512
CHUNK_LOG2 = 6
ROPE_THETA = 10000.0
N_KEYS = 128
PEER_TOPK = 16
LANE = 128
EXPERT_CHUNK = 1024
VMEM_LIMIT_MB = 56


def _cparams(sem, vmem_mb=VMEM_LIMIT_MB):
    return pltpu.CompilerParams(dimension_semantics=sem, vmem_limit_bytes=vmem_mb * 1024 * 1024)


def _mod_rows(ref):
    v = ref[...]
    return v[0] if v.ndim == 3 else v


def _mod_spec(mod, tm, seq):
    if mod.ndim == 3:
        blocks_per_batch = seq // tm
        return pl.BlockSpec((1, 1, D_MODEL), lambda i, *_: (i // blocks_per_batch, 0, 0))
    return pl.BlockSpec((tm, D_MODEL), lambda i, *_: (i, 0))


def _rope128(r, c, t1, t2):
    return r * c + pltpu.roll(r, 96, 1) * t1 + pltpu.roll(r, 32, 1) * t2


def _ada_kernel(c_ref, w_ref, b_ref, o_ref):
    c = c_ref[...]
    a = (c * jax.nn.sigmoid(c)).astype(BF16)
    o_ref[...] = jnp.dot(a, w_ref[...].astype(BF16), preferred_element_type=F32) + b_ref[...]


def _ada(c_all, w_ada, b_ada):
    rows, n = c_all.shape[0], w_ada.shape[1]
    tn = 1024
    return pl.pallas_call(
        _ada_kernel,
        grid=(n // tn,),
        in_specs=[pl.BlockSpec((rows, D_MODEL), lambda j: (0, 0)),
                  pl.BlockSpec((D_MODEL, tn), lambda j: (0, j)),
                  pl.BlockSpec((1, tn), lambda j: (0, j))],
        out_specs=pl.BlockSpec((rows, tn), lambda j: (0, j)),
        out_shape=jax.ShapeDtypeStruct((rows, n), F32),
        compiler_params=_cparams(("parallel",)),
        name="ada",
    )(c_all, w_ada, b_ada)


def _norm_mod_kernel(x_ref, g_ref, sh_ref, sc_ref, o_ref):
    x = x_ref[...]
    y = x * lax.rsqrt(jnp.mean(x * x, axis=-1, keepdims=True) + EPS) * g_ref[...]
    o_ref[...] = (y * (1.0 + _mod_rows(sc_ref)) + _mod_rows(sh_ref)).astype(o_ref.dtype)


def _norm_mod(x, g, shift, scale, seq, tm):
    t = x.shape[0]
    return pl.pallas_call(
        _norm_mod_kernel,
        grid=(t // tm,),
        in_specs=[pl.BlockSpec((tm, D_MODEL), lambda i: (i, 0)),
                  pl.BlockSpec((1, D_MODEL), lambda i: (0, 0)),
                  _mod_spec(shift, tm, seq), _mod_spec(scale, tm, seq)],
        out_specs=pl.BlockSpec((tm, D_MODEL), lambda i: (i, 0)),
        out_shape=jax.ShapeDtypeStruct((t, D_MODEL), BF16),
        compiler_params=_cparams(("parallel",)),
        name="norm_mod",
    )(x, g, shift, scale)


def _proj_kernel(h_ref, w_ref, *o_refs, scale, act):
    acc = jnp.dot(h_ref[...], w_ref[...], preferred_element_type=F32)
    if scale != 1.0:
        acc = acc * scale
    if act == "sigmoid":
        acc = jax.nn.sigmoid(acc)
    for o_ref in o_refs:
        o_ref[...] = acc.astype(o_ref.dtype)


def _proj(h, w, out_dtypes, tm, tn, scale=1.0, act=None, name="proj"):
    t, k = h.shape
    n = w.shape[1]
    outs = pl.pallas_call(
        functools.partial(_proj_kernel, scale=scale, act=act),
        grid=(t // tm, n // tn),
        in_specs=[pl.BlockSpec((tm, k), lambda i, j: (i, 0)),
                  pl.BlockSpec((k, tn), lambda i, j: (0, j))],
        out_specs=[pl.BlockSpec((tm, tn), lambda i, j: (i, j)) for _ in out_dtypes],
        out_shape=[jax.ShapeDtypeStruct((t, n), dt) for dt in out_dtypes],
        compiler_params=_cparams(("parallel", "parallel")),
        name=name,
    )(h, w)
    return outs


def _proj_qmla_kernel(h_ref, w_ref, c_ref, t1_ref, t2_ref, o_ref, *, scale):
    acc = jnp.dot(h_ref[...], w_ref[...], preferred_element_type=F32) * scale
    rot = _rope128(acc[:, LANE:], c_ref[...], t1_ref[...], t2_ref[...])
    o_ref[:, :LANE] = acc[:, :LANE].astype(o_ref.dtype)
    o_ref[:, LANE:] = rot.astype(o_ref.dtype)


def _proj_qmla(h, w, tabs, tm, scale):
    t = h.shape[0]
    ntab = tabs[0].shape[0] // tm
    tab_spec = pl.BlockSpec((tm, LANE), lambda i, j: (i % ntab, 0))
    return pl.pallas_call(
        functools.partial(_proj_qmla_kernel, scale=scale),
        grid=(t // tm, HEADS),
        in_specs=[pl.BlockSpec((tm, D_MODEL), lambda i, j: (i, 0)),
                  pl.BlockSpec((D_MODEL, 2 * LANE), lambda i, j: (0, j)),
                  tab_spec, tab_spec, tab_spec],
        out_specs=pl.BlockSpec((tm, 2 * LANE), lambda i, j: (i, j)),
        out_shape=jax.ShapeDtypeStruct((t, HEADS * 2 * LANE), BF16),
        compiler_params=_cparams(("parallel", "parallel")),
        name="proj_qmla",
    )(h, w, *tabs)


def _proj_kv_kernel(h_ref, w_ref, g_ref, fb_ref, c_ref, t1_ref, t2_ref,
                    lat_ref, latb_ref, kr_ref, krp_ref, logf_ref):
    acc = jnp.dot(h_ref[...], w_ref[...], preferred_element_type=F32)
    ckv = acc[:, :KV_LORA]
    lat = ckv * lax.rsqrt(jnp.mean(ckv * ckv, axis=-1, keepdims=True) + EPS) * g_ref[...]
    lat_ref[...] = lat
    latb_ref[...] = lat.astype(latb_ref.dtype)
    rot = _rope128(acc[:, KV_LORA:KV_LORA + LANE], c_ref[...], t1_ref[...], t2_ref[...])
    kr_ref[...] = rot[:, :QK_ROPE]
    krp_ref[...] = rot.astype(krp_ref.dtype)
    f = acc[:, KV_LORA + LANE:] + fb_ref[...]
    logf = jnp.minimum(f, 0.0) - jnp.log1p(jnp.exp(-jnp.abs(f)))
    logf_ref[...] = logf[:, :HEADS]


def _proj_kv(h, w, g, fb, tabs, tm):
    t = h.shape[0]
    n = w.shape[1]
    ntab = tabs[0].shape[0] // tm
    tab_spec = pl.BlockSpec((tm, LANE), lambda i: (i % ntab, 0))
    row = lambda width: pl.BlockSpec((tm, width), lambda i: (i, 0))
    return pl.pallas_call(
        _proj_kv_kernel,
        grid=(t // tm,),
        in_specs=[row(D_MODEL),
                  pl.BlockSpec((D_MODEL, n), lambda i: (0, 0)),
                  pl.BlockSpec((1, KV_LORA), lambda i: (0, 0)),
                  pl.BlockSpec((1, LANE), lambda i: (0, 0)),
                  tab_spec, tab_spec, tab_spec],
        out_specs=[row(KV_LORA), row(KV_LORA), row(QK_ROPE), row(LANE), row(HEADS)],
        out_shape=[jax.ShapeDtypeStruct((t, KV_LORA), F32),
                   jax.ShapeDtypeStruct((t, KV_LORA), BF16),
                   jax.ShapeDtypeStruct((t, QK_ROPE), F32),
                   jax.ShapeDtypeStruct((t, LANE), BF16),
                   jax.ShapeDtypeStruct((t, HEADS), F32)],
        compiler_params=_cparams(("parallel",)),
        name="proj_kv",
    )(h, w, g, fb, *tabs)


def _cumsum_kernel(x_ref, o_ref, carry_ref, *, blk):
    @pl.when(pl.program_id(1) == 0)
    def _():
        carry_ref[...] = jnp.zeros_like(carry_ref)

    x = x_ref[0]
    hi = x.astype(BF16)
    r1 = x - hi.astype(F32)
    mid = r1.astype(BF16)
    lo = (r1 - mid.astype(F32)).astype(BF16)
    rows = lax.broadcasted_iota(jnp.int32, (blk, blk), 0)
    cols = lax.broadcasted_iota(jnp.int32, (blk, blk), 1)
    tril = jnp.where(cols <= rows, 1.0, 0.0).astype(BF16)
    y = (jnp.dot(tril, lo, preferred_element_type=F32)
         + jnp.dot(tril, mid, preferred_element_type=F32)
         + jnp.dot(tril, hi, preferred_element_type=F32)) + carry_ref[...]
    o_ref[0] = y
    carry_ref[...] = y[blk - 1:blk, :]


def _cumsum(logf, blk):
    b, l, h = logf.shape
    return pl.pallas_call(
        functools.partial(_cumsum_kernel, blk=blk),
        grid=(b, l // blk),
        in_specs=[pl.BlockSpec((1, blk, h), lambda bi, i: (bi, i, 0))],
        out_specs=pl.BlockSpec((1, blk, h), lambda bi, i: (bi, i, 0)),
        out_shape=jax.ShapeDtypeStruct((b, l, h), F32),
        scratch_shapes=[pltpu.VMEM((1, h), F32)],
        compiler_params=_cparams(("parallel", "arbitrary")),
        name="cumsum",
    )(logf)


def _attn_kernel(*refs, fox, tq, tk, nk, past, chunk_log2, kv_len, l_pad, dq):
    if fox:
        q_ref, k_ref, v_ref, fq_ref, fk_ref, o_ref, m_sc, l_sc, acc_sc = refs
    else:
        q_ref, k_ref, v_ref, kr_ref, o_ref, m_sc, l_sc, acc_sc = refs
    qi = pl.program_id(1)
    ki = pl.program_id(2)
    q_lo = past + qi * tq
    q_hi = q_lo + (tq - 1)
    k_lo = ki * tk
    k_hi = k_lo + (tk - 1)
    if nk == 1:
        last_k = 0
    else:
        last_k = jnp.minimum(nk - 1, ((((q_hi >> chunk_log2) + 1) << chunk_log2) - 1) // tk)
    needed = ki <= last_k
    full = jnp.logical_and((k_hi >> chunk_log2) <= (q_lo >> chunk_log2), k_hi < kv_len)

    @pl.when(ki == 0)
    def _init():
        m_sc[...] = jnp.full_like(m_sc, NEG_INF)
        l_sc[...] = jnp.zeros_like(l_sc)
        acc_sc[...] = jnp.zeros_like(acc_sc)

    def step(masked):
        if masked:
            rows = q_lo + lax.broadcasted_iota(jnp.int32, (tq, tk), 0)
            cols = k_lo + lax.broadcasted_iota(jnp.int32, (tq, tk), 1)
            ok = (cols >> chunk_log2) <= (rows >> chunk_log2)
            if l_pad != kv_len:
                ok = jnp.logical_and(ok, cols < kv_len)
        for h in range(HEADS):
            hs = slice(h * HEAD_DIM, (h + 1) * HEAD_DIM)
            q = q_ref[0, :, h * dq:(h + 1) * dq]
            k = k_ref[0, :, hs]
            if not fox:
                k = jnp.concatenate([k, kr_ref[0]], axis=-1)
            s = lax.dot_general(q, k, (((1,), (1,)), ((), ())), preferred_element_type=F32)
            if fox:
                s = s + (fq_ref[0, :, h:h + 1] - fk_ref[0, h:h + 1, :])
            if masked:
                s = jnp.where(ok, s, NEG_INF)
            m_prev = m_sc[h]
            m_new = jnp.maximum(m_prev, jnp.max(s, axis=-1, keepdims=True))
            alpha = jnp.exp(m_prev - m_new)
            p = jnp.exp(s - m_new)
            l_sc[h] = alpha * l_sc[h] + jnp.sum(p, axis=-1, keepdims=True)
            acc_sc[:, hs] = alpha * acc_sc[:, hs] + jnp.dot(
                p.astype(BF16), v_ref[0, :, hs], preferred_element_type=F32)
            m_sc[h] = m_new

    @pl.when(jnp.logical_and(needed, full))
    def _full():
        step(False)

    @pl.when(jnp.logical_and(needed, jnp.logical_not(full)))
    def _masked():
        step(True)

    @pl.when(ki == last_k)
    def _fin():
        for h in range(HEADS):
            hs = slice(h * HEAD_DIM, (h + 1) * HEAD_DIM)
            o_ref[0, :, hs] = (acc_sc[:, hs] / l_sc[h]).astype(o_ref.dtype)


def _attention(q, k, v, extra, *, fox, tq, tk, past, kv_len, v_col=0):
    b, sq, qcols = q.shape
    l_pad = k.shape[1]
    dq = qcols // HEADS
    nq, nk = sq // tq, l_pad // tk
    chunk_log2 = 0 if fox else CHUNK_LOG2
    width = HEADS * HEAD_DIM

    def kmap(bi, qi, ki):
        if nk == 1:
            return 0
        q_hi = past + qi * tq + (tq - 1)
        last_k = jnp.minimum(nk - 1, ((((q_hi >> chunk_log2) + 1) << chunk_log2) - 1) // tk)
        return jnp.minimum(ki, last_k)

    in_specs = [pl.BlockSpec((1, tq, qcols), lambda bi, qi, ki: (bi, qi, 0)),
                pl.BlockSpec((1, tk, width), lambda bi, qi, ki: (bi, kmap(bi, qi, ki), 0)),
                pl.BlockSpec((1, tk, width), lambda bi, qi, ki: (bi, kmap(bi, qi, ki), v_col))]
    if fox:
        q_blk_off = past // tq
        in_specs += [pl.BlockSpec((1, tq, HEADS), lambda bi, qi, ki: (bi, q_blk_off + qi, 0)),
                     pl.BlockSpec((1, HEADS, tk), lambda bi, qi, ki: (bi, 0, kmap(bi, qi, ki)))]
    else:
        in_specs += [pl.BlockSpec((1, tk, LANE), lambda bi, qi, ki: (bi, kmap(bi, qi, ki), 0))]
    return pl.pallas_call(
        functools.partial(_attn_kernel, fox=fox, tq=tq, tk=tk, nk=nk, past=past,
                          chunk_log2=chunk_log2, kv_len=kv_len, l_pad=l_pad, dq=dq),
        grid=(b, nq, nk),
        in_specs=in_specs,
        out_specs=pl.BlockSpec((1, tq, width), lambda bi, qi, ki: (bi, qi, 0)),
        out_shape=jax.ShapeDtypeStruct((b, sq, width), BF16),
        scratch_shapes=[pltpu.VMEM((HEADS, tq, 1), F32),
                        pltpu.VMEM((HEADS, tq, 1), F32),
                        pltpu.VMEM((tq, width), F32)],
        compiler_params=_cparams(("parallel", "parallel", "arbitrary")),
        name="attn_fox" if fox else "attn_mla",
    )(q, k, v, *extra)


def _merge_kernel(om_ref, of_ref, wm_ref, wf_ref, gm_ref, gf_ref, o_ref):
    a = jnp.dot(om_ref[...], wm_ref[...], preferred_element_type=F32)
    b = jnp.dot(of_ref[...], wf_ref[...], preferred_element_type=F32)
    o_ref[...] = (gm_ref[...].astype(F32) * a + gf_ref[...].astype(F32) * b).astype(o_ref.dtype)


def _merge(o_mla, o_fox, w_o_mla, w_o_fox, gates, tm, tn):
    t, width = o_mla.shape
    ncol = D_MODEL // tn
    return pl.pallas_call(
        _merge_kernel,
        grid=(t // tm, ncol),
        in_specs=[pl.BlockSpec((tm, width), lambda i, j: (i, 0)),
                  pl.BlockSpec((tm, width), lambda i, j: (i, 0)),
                  pl.BlockSpec((width, tn), lambda i, j: (0, j)),
                  pl.BlockSpec((width, tn), lambda i, j: (0, j)),
                  pl.BlockSpec((tm, tn), lambda i, j: (i, j)),
                  pl.BlockSpec((tm, tn), lambda i, j: (i, ncol + j))],
        out_specs=pl.BlockSpec((tm, tn), lambda i, j: (i, j)),
        out_shape=jax.ShapeDtypeStruct((t, D_MODEL), BF16),
        compiler_params=_cparams(("parallel", "parallel")),
        name="merge",
    )(o_mla, o_fox, w_o_mla, w_o_fox, gates, gates)


def _outproj_kernel(x_ref, mg_ref, w_ref, g1_ref, n2_ref, sh_ref, sc_ref, x1_ref, h2_ref):
    y = jnp.dot(mg_ref[...], w_ref[...], preferred_element_type=F32)
    x1 = x_ref[...] + _mod_rows(g1_ref) * y
    x1_ref[...] = x1
    n = x1 * lax.rsqrt(jnp.mean(x1 * x1, axis=-1, keepdims=True) + EPS) * n2_ref[...]
    h2_ref[...] = (n * (1.0 + _mod_rows(sc_ref)) + _mod_rows(sh_ref)).astype(h2_ref.dtype)


def _outproj(x, merged, w_out, gate1, norm2_g, shift2, scale2, seq, tm):
    t = x.shape[0]
    row = pl.BlockSpec((tm, D_MODEL), lambda i: (i, 0))
    return pl.pallas_call(
        _outproj_kernel,
        grid=(t // tm,),
        in_specs=[row, row,
                  pl.BlockSpec((D_MODEL, D_MODEL), lambda i: (0, 0)),
                  _mod_spec(gate1, tm, seq),
                  pl.BlockSpec((1, D_MODEL), lambda i: (0, 0)),
                  _mod_spec(shift2, tm, seq), _mod_spec(scale2, tm, seq)],
        out_specs=[row, row],
        out_shape=[jax.ShapeDtypeStruct((t, D_MODEL), F32),
                   jax.ShapeDtypeStruct((t, D_MODEL), BF16)],
        compiler_params=_cparams(("parallel",)),
        name="outproj",
    )(x, merged, w_out, gate1, norm2_g, shift2, scale2)


def _peer_score_kernel(h_ref, wq_ref, k1_ref, k2_ref, s1_ref, s2_ref):
    q = jnp.dot(h_ref[...], wq_ref[...], preferred_element_type=F32).astype(BF16)
    nt = (((1,), (1,)), ((), ()))
    half = N_KEYS
    for h in range(HEADS):
        base = h * 2 * half
        s1_ref[h] = lax.dot_general(k1_ref[...], q[:, base:base + half], nt,
                                    preferred_element_type=F32)
        s2_ref[h] = lax.dot_general(k2_ref[...], q[:, base + half:base + 2 * half], nt,
                                    preferred_element_type=F32)


def _peer_scores(h2, wq, k1, k2, tm):
    t = h2.shape[0]
    out = jax.ShapeDtypeStruct((HEADS, N_KEYS, t), F32)
    ospec = pl.BlockSpec((HEADS, N_KEYS, tm), lambda i: (0, 0, i))
    kspec = pl.BlockSpec((N_KEYS, N_KEYS), lambda i: (0, 0))
    return pl.pallas_call(
        _peer_score_kernel,
        grid=(t // tm,),
        in_specs=[pl.BlockSpec((tm, D_MODEL), lambda i: (i, 0)),
                  pl.BlockSpec((D_MODEL, D_MODEL), lambda i: (0, 0)),
                  kspec, kspec],
        out_specs=[ospec, ospec],
        out_shape=[out, out],
        compiler_params=_cparams(("parallel",)),
        name="peer_scores",
    )(h2, wq, k1, k2)


_N_RANK = PEER_TOPK + 1
_CAND_PAIRS = [(i, j) for i in range(_N_RANK) for j in range(_N_RANK) if (i + 1) * (j + 1) <= _N_RANK]
_CAND_ROWS = -(-len(_CAND_PAIRS) // 8) * 8


def _top_values(x, n):
    vals = []
    for _ in range(n):
        mx = jnp.max(x, axis=0, keepdims=True)
        vals.append(mx)
        x = jnp.where(x == mx, -jnp.inf, x)
    return vals


def _peer_topk_kernel(s1_ref, s2_ref, c1_ref, d1_ref, e2_ref, cand_sc):
    for h in range(HEADS):
        a = s1_ref[h]
        b = s2_ref[h]
        v1 = _top_values(a, _N_RANK)
        v2 = _top_values(b, _N_RANK)
        cand_sc[...] = jnp.full_like(cand_sc, -jnp.inf)
        for r, (i, j) in enumerate(_CAND_PAIRS):
            cand_sc[r:r + 1, :] = v1[i] + v2[j]
        tops = _top_values(cand_sc[...], _N_RANK)
        tau = 0.5 * (tops[PEER_TOPK - 1] + tops[PEER_TOPK])
        z = jnp.exp(tops[0] - tops[0])
        for t in tops[1:PEER_TOPK]:
            z = z + jnp.exp(t - tops[0])
        c1_ref[h] = jnp.exp(a - v1[0]) / z
        d1_ref[h] = tau - a
        e2_ref[h] = jnp.exp(b - v2[0])


def _peer_topk(s1t, s2t, tl):
    t = s1t.shape[2]
    spec = pl.BlockSpec((HEADS, N_KEYS, tl), lambda i: (0, 0, i))
    out = jax.ShapeDtypeStruct(s1t.shape, F32)
    return pl.pallas_call(
        _peer_topk_kernel,
        grid=(t // tl,),
        in_specs=[spec, spec],
        out_specs=[spec, spec, spec],
        out_shape=[out, out, out],
        scratch_shapes=[pltpu.VMEM((_CAND_ROWS, tl), F32)],
        compiler_params=_cparams(("parallel",)),
        name="peer_topk",
    )(s1t, s2t)


def _peer_dense_kernel(h_ref, u_ref, vt_ref, c1_ref, d1_ref, s2_ref, e2_ref, o_ref, w_sc, acc_sc):
    c = pl.program_id(1)

    @pl.when(c == 0)
    def _():
        acc_sc[...] = jnp.zeros_like(acc_sc)

    at = lax.dot_general(u_ref[...], h_ref[...], (((1,), (1,)), ((), ())),
                         preferred_element_type=F32)
    for ii in range(EXPERT_CHUNK // N_KEYS):
        a = at[ii * N_KEYS:(ii + 1) * N_KEYS, :]
        g = jnp.zeros_like(a)
        for h in range(HEADS):
            d = d1_ref[h, ii:ii + 1, :]
            cf = c1_ref[h, ii:ii + 1, :]
            g = g + jnp.where(s2_ref[h] >= d, e2_ref[h] * cf, 0.0)
        gelu = a * (lax.erf(a / math.sqrt(2.0)) + 1.0) / 2.0
        w_sc[ii * N_KEYS:(ii + 1) * N_KEYS, :] = (g * gelu).astype(w_sc.dtype)
    acc_sc[...] += jnp.dot(vt_ref[...], w_sc[...], preferred_element_type=F32)

    @pl.when(c == pl.num_programs(1) - 1)
    def _():
        o_ref[...] = acc_sc[...].T


def _peer_dense(h2, u_bf, vt_bf, c1t, d1t, s2t, e2t, tb):
    t = h2.shape[0]
    n_exp = u_bf.shape[0]
    rows = EXPERT_CHUNK // N_KEYS
    small = pl.BlockSpec((HEADS, rows, tb), lambda i, c: (0, c, i))
    big = pl.BlockSpec((HEADS, N_KEYS, tb), lambda i, c: (0, 0, i))
    return pl.pallas_call(
        _peer_dense_kernel,
        grid=(t // tb, n_exp // EXPERT_CHUNK),
        in_specs=[pl.BlockSpec((tb, D_MODEL), lambda i, c: (i, 0)),
                  pl.BlockSpec((EXPERT_CHUNK, D_MODEL), lambda i, c: (c, 0)),
                  pl.BlockSpec((D_MODEL, EXPERT_CHUNK), lambda i, c: (0, c)),
                  small, small, big, big],
        out_specs=pl.BlockSpec((tb, D_MODEL), lambda i, c: (i, 0)),
        out_shape=jax.ShapeDtypeStruct((t, D_MODEL), F32),
        scratch_shapes=[pltpu.VMEM((EXPERT_CHUNK, tb), BF16),
                        pltpu.VMEM((D_MODEL, tb), F32)],
        compiler_params=_cparams(("parallel", "arbitrary")),
        name="peer_dense",
    )(h2, u_bf, vt_bf, c1t, d1t, s2t, e2t)


def _final_kernel(x_ref, p_ref, g2_ref, fg_ref, o_ref):
    x = x_ref[...] + _mod_rows(g2_ref) * p_ref[...]
    o_ref[...] = x * lax.rsqrt(jnp.mean(x * x, axis=-1, keepdims=True) + EPS) * fg_ref[...]


def _final(x1, peer, gate2, final_g, seq, tm):
    t = x1.shape[0]
    row = pl.BlockSpec((tm, D_MODEL), lambda i: (i, 0))
    return pl.pallas_call(
        _final_kernel,
        grid=(t // tm,),
        in_specs=[row, row, _mod_spec(gate2, tm, seq), pl.BlockSpec((1, D_MODEL), lambda i: (0, 0))],
        out_specs=row,
        out_shape=jax.ShapeDtypeStruct((t, D_MODEL), F32),
        compiler_params=_cparams(("parallel",)),
        name="final",
    )(x1, peer, gate2, final_g)


def _rope_tables(pos):
    inv = 1.0 / (ROPE_THETA ** (jnp.arange(0, QK_ROPE, 2, dtype=F32) / QK_ROPE))
    ang = pos.astype(F32)[:, None] * inv[None, :]
    cos, sin = jnp.cos(ang), jnp.sin(ang)
    z = jnp.zeros_like(cos)
    return (jnp.concatenate([cos, cos, z, z], axis=1),
            jnp.concatenate([-sin, z, z, z], axis=1),
            jnp.concatenate([z, sin, z, z], axis=1))


def _prep_weights(w_in, fox_fbias, w_ukv, w_o_mla, w_o_fox, w_out, peer_wq, peer_keys1, peer_keys2,
                  peer_u, peer_v):
    d = D_MODEL
    n_q = HEADS * (HEAD_DIM + QK_ROPE)
    wq = w_in[:, :n_q].reshape(d, HEADS, HEAD_DIM + QK_ROPE)
    wq_cat = jnp.concatenate([wq, jnp.zeros((d, HEADS, QK_ROPE), F32)], axis=-1).reshape(d, HEADS * 2 * LANE)
    o1 = n_q + KV_LORA
    o2 = o1 + QK_ROPE
    o3 = o2 + 3 * HEADS * HEAD_DIM
    o4 = o3 + HEADS
    w_kv = jnp.concatenate([w_in[:, n_q:o2], jnp.zeros((d, LANE - QK_ROPE), F32),
                            w_in[:, o3:o4], jnp.zeros((d, LANE - HEADS), F32)], axis=1)
    fb = jnp.concatenate([fox_fbias, jnp.zeros((LANE - HEADS,), F32)])[None, :]
    w_ukv3 = w_ukv.reshape(KV_LORA, HEADS, 2 * HEAD_DIM)
    w_ukv_r = jnp.concatenate([w_ukv3[:, :, :HEAD_DIM].reshape(KV_LORA, -1),
                               w_ukv3[:, :, HEAD_DIM:].reshape(KV_LORA, -1)], axis=1)
    return dict(
        wq_cat=wq_cat.astype(BF16), w_kv=w_kv.astype(BF16), fb=fb,
        w_fox=w_in[:, o2:o3].astype(BF16), w_gate=w_in[:, o4:].astype(BF16),
        w_ukv=w_ukv_r.astype(BF16), w_o_mla=w_o_mla.astype(BF16), w_o_fox=w_o_fox.astype(BF16),
        w_out=w_out.astype(BF16), peer_wq=peer_wq.astype(BF16),
        k1=peer_keys1.astype(BF16), k2=peer_keys2.astype(BF16),
        u=peer_u.astype(BF16), vt=peer_v.T.astype(BF16))


def _layer(x, mods, caches, wts, norm1_g, kv_norm_g, norm2_g, final_g, *, batch, seq, past, tm, tq, tk):
    t = batch * seq
    shift1, scale1, gate1, shift2, scale2, gate2 = mods
    width = HEADS * HEAD_DIM
    pos = past + jnp.arange(seq, dtype=jnp.int32)
    tabs = _rope_tables(pos)
    if tm > seq:
        tabs = tuple(jnp.tile(tb, (tm // seq, 1)) for tb in tabs)

    h = _norm_mod(x, norm1_g, shift1, scale1, seq, min(tm, 512))
    q_mla = _proj_qmla(h, wts["wq_cat"], tabs, tm, (HEAD_DIM + QK_ROPE) ** -0.5)
    lat, lat_b, k_r, k_rp, logf = _proj_kv(h, wts["w_kv"], kv_norm_g, wts["fb"], tabs, tm)
    w_fox = wts["w_fox"]
    (q_fox,) = _proj(h, w_fox[:, :width], [BF16], tm, 512, scale=HEAD_DIM ** -0.5, name="proj_foxq")
    k_new, k_new_b = _proj(h, w_fox[:, width:2 * width], [F32, BF16], tm, 512, name="proj_foxk")
    v_new, v_new_b = _proj(h, w_fox[:, 2 * width:], [F32, BF16], tm, 512, name="proj_foxv")
    (gates,) = _proj(h, wts["w_gate"], [BF16], tm, 512, act="sigmoid", name="proj_gate")

    kv_len = past + seq
    l_pad = -(-kv_len // tk) * tk

    def with_cache(cache, new, cols):
        new = new.reshape(batch, seq, cols)
        if cache is None:
            return new
        parts = [cache.reshape(batch, past, cols).astype(new.dtype), new]
        if l_pad > kv_len:
            parts.append(jnp.zeros((batch, l_pad - kv_len, cols), new.dtype))
        return jnp.concatenate(parts, axis=1)

    c_lat, c_kr, c_k, c_v, c_logf = caches if caches is not None else (None,) * 5
    lat_all = with_cache(c_lat, lat_b, KV_LORA)
    kr_all = with_cache(None if c_kr is None else jnp.pad(c_kr, ((0, 0), (0, 0), (0, LANE - QK_ROPE))),
                        k_rp, LANE)
    k_all = with_cache(c_k, k_new_b, width)
    v_all = with_cache(c_v, v_new_b, width)
    logf_all = with_cache(c_logf, logf, HEADS)

    (kv_up,) = _proj(lat_all.reshape(batch * l_pad, KV_LORA), wts["w_ukv"], [BF16],
                     min(1024, batch * l_pad), 512, name="kv_up")
    kv_up = kv_up.reshape(batch, l_pad, 2 * width)
    o_mla = _attention(q_mla.reshape(batch, seq, -1), kv_up, kv_up, (kr_all,), fox=False,
                       tq=tq, tk=tk, past=past, kv_len=kv_len, v_col=1)

    f_cum = _cumsum(logf_all, tk)
    o_fox = _attention(q_fox.reshape(batch, seq, width), k_all, v_all,
                       (f_cum, jnp.swapaxes(f_cum, 1, 2)), fox=True,
                       tq=tq, tk=tk, past=past, kv_len=kv_len)

    merged = _merge(o_mla.reshape(t, width), o_fox.reshape(t, width), wts["w_o_mla"], wts["w_o_fox"],
                    gates, tm, 512)
    tm2 = 512 if gate1.ndim == 3 else 256
    x1, h2 = _outproj(x, merged, wts["w_out"], gate1, norm2_g, shift2, scale2, seq, tm2)

    s1t, s2t = _peer_scores(h2, wts["peer_wq"], wts["k1"], wts["k2"], tm2)
    c1t, d1t, e2t = _peer_topk(s1t, s2t, 256)
    peer = _peer_dense(h2, wts["u"], wts["vt"], c1t, d1t, s2t, e2t, 512)
    y = _final(x1, peer, gate2, final_g, seq, tm2)
    return y, (lat, k_r, k_new, v_new, logf)


def kernel(x_prompt, x_sample, c_prompt, c_sample, cache_mla_latent, cache_mla_krope, cache_fox_k,
           cache_fox_v, cache_fox_logf, w_ada, b_ada, norm1_g, w_in, fox_fbias, kv_norm_g, w_ukv,
           w_o_mla, w_o_fox, w_out, norm2_g, peer_wq, peer_keys1, peer_keys2, peer_u, peer_v, final_g):
    depth = w_ada.shape[0]
    assert depth == 1, "the final RMSNorm is fused into the single trunk layer"
    bp, sp, d = x_prompt.shape
    bs, ss, _ = x_sample.shape
    past = cache_mla_latent.shape[2]
    xp = x_prompt.reshape(bp * sp, d)
    xs = x_sample.reshape(bs * ss, d)
    c_rows = -(-(bp + bs) // 16) * 16
    c_all = jnp.concatenate([c_prompt, c_sample, jnp.zeros((c_rows - bp - bs, d), F32)], axis=0)
    fg = final_g[None, :]
    states_p, states_s = [], []
    for l in range(depth):
        wts = _prep_weights(w_in[l], fox_fbias[l], w_ukv[l], w_o_mla[l], w_o_fox[l], w_out[l],
                            peer_wq[l], peer_keys1[l], peer_keys2[l], peer_u[l], peer_v[l])
        mod = _ada(c_all, w_ada[l], b_ada[l][None, :])
        mods_p = tuple(m[:, None, :] for m in jnp.split(mod[:bp], N_MOD, axis=-1))
        mods_s = tuple(jnp.repeat(m, ss, axis=0) for m in jnp.split(mod[bp:bp + bs], N_MOD, axis=-1))
        n1, kvg, n2 = norm1_g[l][None, :], kv_norm_g[l][None, :], norm2_g[l][None, :]
        xp, st_p = _layer(xp, mods_p, None, wts, n1, kvg, n2, fg,
                          batch=bp, seq=sp, past=0, tm=1024, tq=512, tk=512)
        caches = (cache_mla_latent[l], cache_mla_krope[l], cache_fox_k[l], cache_fox_v[l], cache_fox_logf[l])
        xs, st_s = _layer(xs, mods_s, caches, wts, n1, kvg, n2, fg,
                          batch=bs, seq=ss, past=past, tm=1024, tq=ss, tk=1152)
        states_p.append(st_p)
        states_s.append(st_s)

    def stack(states, i, batch, seq, tail):
        return jnp.stack([st[i].reshape((batch, seq) + tail) for st in states])

    tails = [(KV_LORA,), (QK_ROPE,), (HEADS, HEAD_DIM), (HEADS, HEAD_DIM), (HEADS,)]
    outs_p = [stack(states_p, i, bp, sp, tails[i]) for i in range(5)]
    outs_s = [stack(states_s, i, bs, ss, tails[i]) for i in range(5)]
    return (xp.reshape(bp, sp, d), xs.reshape(bs, ss, d), *outs_p, *outs_s)
```

```python
import functools
import math

import jax
import jax.numpy as jnp
import numpy as np
from jax import lax
from jax.experimental import pallas as pl
from jax.experimental.pallas import tpu as pltpu

F32 = jnp.float32
BF16 = jnp.bfloat16

D_MODEL = 2048
EPS = 1e-6
NEG_INF = -1e30
LOG2E = 1.4426950408889634
N_MOD = 6
HEADS = 8
HEAD_DIM = 128
QK_ROPE = 64
KV_LORA = 512
CHUNK_LOG2 = 6
ROPE_THETA = 10000.0
N_KEYS = 128
PEER_TOPK = 16
LANE = 128
EXPERT_CHUNK = 1024
VMEM_LIMIT_MB = 56


def _cparams(sem, vmem_mb=VMEM_LIMIT_MB, flags=None):
    return pltpu.CompilerParams(dimension_semantics=sem, vmem_limit_bytes=vmem_mb * 1024 * 1024,
                                flags=flags)


def _mod_rows(ref):
    v = ref[...]
    return v[0] if v.ndim == 3 else v


def _mod_spec(mod, tm, seq):
    if mod.ndim == 3:
        blocks_per_batch = seq // tm
        return pl.BlockSpec((1, 1, D_MODEL), lambda i, *_: (i // blocks_per_batch, 0, 0))
    return pl.BlockSpec((tm, D_MODEL), lambda i, *_: (i, 0))


def _rope128(r, c, t1, t2):
    return r * c + pltpu.roll(r, 96, 1) * t1 + pltpu.roll(r, 32, 1) * t2


def _ada_kernel(c_ref, w_ref, b_ref, o_ref):
    c = c_ref[...]
    a = (c * jax.nn.sigmoid(c)).astype(BF16)
    o_ref[...] = jnp.dot(a, w_ref[...].astype(BF16), preferred_element_type=F32) + b_ref[...]


def _ada(c_all, w_ada, b_ada):
    rows, n = c_all.shape[0], w_ada.shape[1]
    tn = 1024
    return pl.pallas_call(
        _ada_kernel,
        grid=(n // tn,),
        in_specs=[pl.BlockSpec((rows, D_MODEL), lambda j: (0, 0)),
                  pl.BlockSpec((D_MODEL, tn), lambda j: (0, j)),
                  pl.BlockSpec((1, tn), lambda j: (0, j))],
        out_specs=pl.BlockSpec((rows, tn), lambda j: (0, j)),
        out_shape=jax.ShapeDtypeStruct((rows, n), F32),
        compiler_params=_cparams(("parallel",)),
        name="ada",
    )(c_all, w_ada, b_ada)


def _norm_mod_kernel(x_ref, g_ref, sh_ref, sc_ref, o_ref):
    x = x_ref[...]
    y = x * lax.rsqrt(jnp.mean(x * x, axis=-1, keepdims=True) + EPS) * g_ref[...]
    o_ref[...] = (y * (1.0 + _mod_rows(sc_ref)) + _mod_rows(sh_ref)).astype(o_ref.dtype)


def _norm_mod(x, g, shift, scale, seq, tm):
    t = x.shape[0]
    return pl.pallas_call(
        _norm_mod_kernel,
        grid=(t // tm,),
        in_specs=[pl.BlockSpec((tm, D_MODEL), lambda i: (i, 0)),
                  pl.BlockSpec((1, D_MODEL), lambda i: (0, 0)),
                  _mod_spec(shift, tm, seq), _mod_spec(scale, tm, seq)],
        out_specs=pl.BlockSpec((tm, D_MODEL), lambda i: (i, 0)),
        out_shape=jax.ShapeDtypeStruct((t, D_MODEL), BF16),
        compiler_params=_cparams(("parallel",)),
        name="norm_mod",
    )(x, g, shift, scale)


def _proj_kernel(h_ref, w_ref, *o_refs, scale, act):
    acc = jnp.dot(h_ref[...], w_ref[...], preferred_element_type=F32)
    if scale != 1.0:
        acc = acc * scale
    if act == "sigmoid":
        acc = jax.nn.sigmoid(acc)
    for o_ref in o_refs:
        o_ref[...] = acc.astype(o_ref.dtype)


def _proj(h, w, out_dtypes, tm, tn, scale=1.0, act=None, name="proj"):
    t, k = h.shape
    n = w.shape[1]
    outs = pl.pallas_call(
        functools.partial(_proj_kernel, scale=scale, act=act),
        grid=(t // tm, n // tn),
        in_specs=[pl.BlockSpec((tm, k), lambda i, j: (i, 0)),
                  pl.BlockSpec((k, tn), lambda i, j: (0, j))],
        out_specs=[pl.BlockSpec((tm, tn), lambda i, j: (i, j)) for _ in out_dtypes],
        out_shape=[jax.ShapeDtypeStruct((t, n), dt) for dt in out_dtypes],
        compiler_params=_cparams(("parallel", "parallel")),
        name=name,
    )(h, w)
    return outs


def _proj_qmla_kernel(h_ref, w_ref, c_ref, t1_ref, t2_ref, o_ref, *, scale):
    acc = jnp.dot(h_ref[...], w_ref[...], preferred_element_type=F32) * scale
    rot = _rope128(acc[:, LANE:], c_ref[...], t1_ref[...], t2_ref[...])
    o_ref[:, :LANE] = acc[:, :LANE].astype(o_ref.dtype)
    o_ref[:, LANE:] = rot.astype(o_ref.dtype)


def _proj_qmla(h, w, tabs, tm, scale):
    t = h.shape[0]
    ntab = tabs[0].shape[0] // tm
    tab_spec = pl.BlockSpec((tm, LANE), lambda i, j: (i % ntab, 0))
    return pl.pallas_call(
        functools.partial(_proj_qmla_kernel, scale=scale),
        grid=(t // tm, HEADS),
        in_specs=[pl.BlockSpec((tm, D_MODEL), lambda i, j: (i, 0)),
                  pl.BlockSpec((D_MODEL, 2 * LANE), lambda i, j: (0, j)),
                  tab_spec, tab_spec, tab_spec],
        out_specs=pl.BlockSpec((tm, 2 * LANE), lambda i, j: (i, j)),
        out_shape=jax.ShapeDtypeStruct((t, HEADS * 2 * LANE), BF16),
        compiler_params=_cparams(("parallel", "parallel")),
        name="proj_qmla",
    )(h, w, *tabs)


def _proj_kv_kernel(h_ref, w_ref, g_ref, fb_ref, c_ref, t1_ref, t2_ref,
                    lat_ref, latb_ref, kr_ref, krp_ref, logf_ref):
    acc = jnp.dot(h_ref[...], w_ref[...], preferred_element_type=F32)
    ckv = acc[:, :KV_LORA]
    lat = ckv * lax.rsqrt(jnp.mean(ckv * ckv, axis=-1, keepdims=True) + EPS) * g_ref[...]
    lat_ref[...] = lat
    latb_ref[...] = lat.astype(latb_ref.dtype)
    rot = _rope128(acc[:, KV_LORA:KV_LORA + LANE], c_ref[...], t1_ref[...], t2_ref[...])
    kr_ref[...] = rot[:, :QK_ROPE]
    krp_ref[...] = rot.astype(krp_ref.dtype)
    f = acc[:, KV_LORA + LANE:] + fb_ref[...]
    logf = jnp.minimum(f, 0.0) - jnp.log1p(jnp.exp(-jnp.abs(f)))
    logf_ref[...] = logf[:, :HEADS]


def _proj_kv(h, w, g, fb, tabs, tm):
    t = h.shape[0]
    n = w.shape[1]
    ntab = tabs[0].shape[0] // tm
    tab_spec = pl.BlockSpec((tm, LANE), lambda i: (i % ntab, 0))
    row = lambda width: pl.BlockSpec((tm, width), lambda i: (i, 0))
    return pl.pallas_call(
        _proj_kv_kernel,
        grid=(t // tm,),
        in_specs=[row(D_MODEL),
                  pl.BlockSpec((D_MODEL, n), lambda i: (0, 0)),
                  pl.BlockSpec((1, KV_LORA), lambda i: (0, 0)),
                  pl.BlockSpec((1, LANE), lambda i: (0, 0)),
                  tab_spec, tab_spec, tab_spec],
        out_specs=[row(KV_LORA), row(KV_LORA), row(QK_ROPE), row(LANE), row(HEADS)],
        out_shape=[jax.ShapeDtypeStruct((t, KV_LORA), F32),
                   jax.ShapeDtypeStruct((t, KV_LORA), BF16),
                   jax.ShapeDtypeStruct((t, QK_ROPE), F32),
                   jax.ShapeDtypeStruct((t, LANE), BF16),
                   jax.ShapeDtypeStruct((t, HEADS), F32)],
        compiler_params=_cparams(("parallel",)),
        name="proj_kv",
    )(h, w, g, fb, *tabs)


def _split3(x):
    hi = x.astype(BF16).astype(F32)
    r1 = x - hi
    mid = r1.astype(BF16).astype(F32)
    lo = (r1 - mid).astype(BF16).astype(F32)
    return hi, mid, lo


def _cumsum_kernel(x_ref, pq_ref, pk_ref, oq_ref, ok_ref, qa_ref, ka_ref, carry_ref, *, blk):
    @pl.when(pl.program_id(1) == 0)
    def _():
        carry_ref[...] = jnp.zeros_like(carry_ref)

    hi, mid, lo = _split3(x_ref[0])
    rows = lax.broadcasted_iota(jnp.int32, (blk, blk), 0)
    cols = lax.broadcasted_iota(jnp.int32, (blk, blk), 1)
    tril = jnp.where(cols <= rows, 1.0, 0.0).astype(BF16)
    y = (jnp.dot(tril, lo.astype(BF16), preferred_element_type=F32)
         + jnp.dot(tril, mid.astype(BF16), preferred_element_type=F32)
         + jnp.dot(tril, hi.astype(BF16), preferred_element_type=F32)) + carry_ref[...]
    carry_ref[...] = y[blk - 1:blk, :]
    pieces = _split3(y * LOG2E)
    qa = oq_ref[...]
    ka = ok_ref[...]
    for j, piece in enumerate(pieces):
        qa = qa + jnp.dot(piece, pq_ref[j], preferred_element_type=F32)
        ka = ka - jnp.dot(piece, pk_ref[j], preferred_element_type=F32)
    qa_ref[0] = qa.astype(qa_ref.dtype)
    ka_ref[0] = ka.astype(ka_ref.dtype)


def _ext_placement():
    width = HEADS * LANE
    pq = np.zeros((3, HEADS, width), np.float32)
    pk = np.zeros((3, HEADS, width), np.float32)
    oq = np.zeros((1, width), np.float32)
    ok = np.zeros((1, width), np.float32)
    for h in range(HEADS):
        for j in range(3):
            pq[j, h, h * LANE + j] = 1.0
            pk[j, h, h * LANE + 3 + j] = 1.0
            oq[0, h * LANE + 3 + j] = 1.0
            ok[0, h * LANE + j] = 1.0
    return jnp.asarray(pq), jnp.asarray(pk), jnp.asarray(oq), jnp.asarray(ok)


def _cumsum_ext(logf, blk):
    b, l, h = logf.shape
    width = HEADS * LANE
    pq, pk, oq, ok = _ext_placement()
    pspec = pl.BlockSpec((3, HEADS, width), lambda bi, i: (0, 0, 0))
    ospec = pl.BlockSpec((1, width), lambda bi, i: (0, 0))
    out = jax.ShapeDtypeStruct((b, l, width), BF16)
    return pl.pallas_call(
        functools.partial(_cumsum_kernel, blk=blk),
        grid=(b, l // blk),
        in_specs=[pl.BlockSpec((1, blk, h), lambda bi, i: (bi, i, 0)), pspec, pspec, ospec, ospec],
        out_specs=[pl.BlockSpec((1, blk, width), lambda bi, i: (bi, i, 0))] * 2,
        out_shape=[out, out],
        scratch_shapes=[pltpu.VMEM((1, h), F32)],
        compiler_params=_cparams(("parallel", "arbitrary")),
        name="cumsum",
    )(logf, pq, pk, oq, ok)


def _attn_kernel(*refs, fox, tq, tk, nk, past, chunk_log2, kv_len, l_pad):
    if fox:
        q_ref, qx_ref, k_ref, kx_ref, v_ref, o_ref, m_sc, acc_sc = refs
    else:
        q_ref, k_ref, kx_ref, v_ref, o_ref, m_sc, acc_sc = refs
    qi = pl.program_id(1)
    ki = pl.program_id(2)
    q_lo = past + qi * tq
    q_hi = q_lo + (tq - 1)
    k_lo = ki * tk
    k_hi = k_lo + (tk - 1)
    if nk == 1:
        last_k = 0
    else:
        last_k = jnp.minimum(nk - 1, ((((q_hi >> chunk_log2) + 1) << chunk_log2) - 1) // tk)
    needed = ki <= last_k
    full = jnp.logical_and((k_hi >> chunk_log2) <= (q_lo >> chunk_log2), k_hi < kv_len)

    @pl.when(ki == 0)
    def _init():
        m_sc[...] = jnp.full_like(m_sc, NEG_INF)
        acc_sc[...] = jnp.zeros_like(acc_sc)

    def step(masked):
        if masked:
            rows = q_lo + lax.broadcasted_iota(jnp.int32, (tq, tk), 0)
            cols = k_lo + lax.broadcasted_iota(jnp.int32, (tq, tk), 1)
            ok = (cols >> chunk_log2) <= (rows >> chunk_log2)
            if l_pad != kv_len:
                ok = jnp.logical_and(ok, cols < kv_len)
        ones_col = jnp.where(lax.broadcasted_iota(jnp.int32, (tk, LANE), 1) == 0, 1.0, 0.0).astype(BF16)
        for h in range(HEADS):
            hs = slice(h * HEAD_DIM, (h + 1) * HEAD_DIM)
            if fox:
                q = jnp.concatenate([q_ref[0, :, hs], qx_ref[0, :, hs]], axis=-1)
                k = jnp.concatenate([k_ref[0, :, hs], kx_ref[0, :, hs]], axis=-1)
            else:
                q = q_ref[0, :, 2 * h * LANE:2 * (h + 1) * LANE]
                k = jnp.concatenate([k_ref[0, :, hs], kx_ref[0]], axis=-1)
            s = lax.dot_general(q, k, (((1,), (1,)), ((), ())), preferred_element_type=F32)
            if masked:
                s = jnp.where(ok, s, NEG_INF)
            m_prev = m_sc[h]
            m_new = jnp.maximum(m_prev, jnp.max(s, axis=-1, keepdims=True))
            alpha = jnp.exp2(m_prev - m_new)
            p = jnp.exp2(s - jnp.tile(m_new, (1, tk // LANE)))
            v = jnp.concatenate([v_ref[0, :, hs], ones_col], axis=-1)
            acc_sc[h] = jnp.tile(alpha, (1, 2)) * acc_sc[h] + jnp.dot(
                p.astype(BF16), v, preferred_element_type=F32)
            m_sc[h] = m_new

    @pl.when(jnp.logical_and(needed, full))
    def _full():
        step(False)

    @pl.when(jnp.logical_and(needed, jnp.logical_not(full)))
    def _masked():
        step(True)

    @pl.when(ki == last_k)
    def _fin():
        for h in range(HEADS):
            hs = slice(h * HEAD_DIM, (h + 1) * HEAD_DIM)
            acc = acc_sc[h]
            o_ref[0, :, hs] = (acc[:, :HEAD_DIM] / acc[:, HEAD_DIM:HEAD_DIM + 1]).astype(o_ref.dtype)


def _attention(q, k, v, extra, *, fox, tq, tk, past, kv_len, v_col=0):
    b, sq, qcols = q.shape
    l_pad = k.shape[1]
    nq, nk = sq // tq, l_pad // tk
    chunk_log2 = 0 if fox else CHUNK_LOG2
    width = HEADS * HEAD_DIM

    def kmap(bi, qi, ki):
        if nk == 1:
            return 0
        q_hi = past + qi * tq + (tq - 1)
        last_k = jnp.minimum(nk - 1, ((((q_hi >> chunk_log2) + 1) << chunk_log2) - 1) // tk)
        return jnp.minimum(ki, last_k)

    q_spec = pl.BlockSpec((1, tq, qcols), lambda bi, qi, ki: (bi, qi, 0))
    k_spec = pl.BlockSpec((1, tk, width), lambda bi, qi, ki: (bi, kmap(bi, qi, ki), 0))
    v_spec = pl.BlockSpec((1, tk, width), lambda bi, qi, ki: (bi, kmap(bi, qi, ki), v_col))
    if fox:
        q_blk_off = past // tq
        qx_spec = pl.BlockSpec((1, tq, width), lambda bi, qi, ki: (bi, q_blk_off + qi, 0))
        in_specs = [q_spec, qx_spec, k_spec, k_spec, v_spec]
        operands = (q, extra[0], k, extra[1], v)
    else:
        kx_spec = pl.BlockSpec((1, tk, LANE), lambda bi, qi, ki: (bi, kmap(bi, qi, ki), 0))
        in_specs = [q_spec, k_spec, kx_spec, v_spec]
        operands = (q, k, extra[0], v)
    return pl.pallas_call(
        functools.partial(_attn_kernel, fox=fox, tq=tq, tk=tk, nk=nk, past=past,
                          chunk_log2=chunk_log2, kv_len=kv_len, l_pad=l_pad),
        grid=(b, nq, nk),
        in_specs=in_specs,
        out_specs=pl.BlockSpec((1, tq, width), lambda bi, qi, ki: (bi, qi, 0)),
        out_shape=jax.ShapeDtypeStruct((b, sq, width), BF16),
        scratch_shapes=[pltpu.VMEM((HEADS, tq, LANE), F32),
                        pltpu.VMEM((HEADS, tq, 2 * LANE), F32)],
        compiler_params=_cparams(("parallel", "parallel", "arbitrary")),
        name="attn_fox" if fox else "attn_mla",
    )(*operands)


def _merge_kernel(om_ref, of_ref, wm_ref, wf_ref, gm_ref, gf_ref, o_ref):
    a = jnp.dot(om_ref[...], wm_ref[...], preferred_element_type=F32)
    b = jnp.dot(of_ref[...], wf_ref[...], preferred_element_type=F32)
    o_ref[...] = (gm_ref[...].astype(F32) * a + gf_ref[...].astype(F32) * b).astype(o_ref.dtype)


def _merge(o_mla, o_fox, w_o_mla, w_o_fox, gates, tm, tn):
    t, width = o_mla.shape
    ncol = D_MODEL // tn
    return pl.pallas_call(
        _merge_kernel,
        grid=(t // tm, ncol),
        in_specs=[pl.BlockSpec((tm, width), lambda i, j: (i, 0)),
                  pl.BlockSpec((tm, width), lambda i, j: (i, 0)),
                  pl.BlockSpec((width, tn), lambda i, j: (0, j)),
                  pl.BlockSpec((width, tn), lambda i, j: (0, j)),
                  pl.BlockSpec((tm, tn), lambda i, j: (i, j)),
                  pl.BlockSpec((tm, tn), lambda i, j: (i, ncol + j))],
        out_specs=pl.BlockSpec((tm, tn), lambda i, j: (i, j)),
        out_shape=jax.ShapeDtypeStruct((t, D_MODEL), BF16),
        compiler_params=_cparams(("parallel", "parallel")),
        name="merge",
    )(o_mla, o_fox, w_o_mla, w_o_fox, gates, gates)


def _outproj_kernel(x_ref, mg_ref, w_ref, g1_ref, n2_ref, sh_ref, sc_ref, x1_ref, h2_ref):
    y = jnp.dot(mg_ref[...], w_ref[...], preferred_element_type=F32)
    x1 = x_ref[...] + _mod_rows(g1_ref) * y
    x1_ref[...] = x1
    n = x1 * lax.rsqrt(jnp.mean(x1 * x1, axis=-1, keepdims=True) + EPS) * n2_ref[...]
    h2_ref[...] = (n * (1.0 + _mod_rows(sc_ref)) + _mod_rows(sh_ref)).astype(h2_ref.dtype)


def _outproj(x, merged, w_out, gate1, norm2_g, shift2, scale2, seq, tm):
    t = x.shape[0]
    row = pl.BlockSpec((tm, D_MODEL), lambda i: (i, 0))
    return pl.pallas_call(
        _outproj_kernel,
        grid=(t // tm,),
        in_specs=[row, row,
                  pl.BlockSpec((D_MODEL, D_MODEL), lambda i: (0, 0)),
                  _mod_spec(gate1, tm, seq),
                  pl.BlockSpec((1, D_MODEL), lambda i: (0, 0)),
                  _mod_spec(shift2, tm, seq), _mod_spec(scale2, tm, seq)],
        out_specs=[row, row],
        out_shape=[jax.ShapeDtypeStruct((t, D_MODEL), F32),
                   jax.ShapeDtypeStruct((t, D_MODEL), BF16)],
        compiler_params=_cparams(("parallel",)),
        name="outproj",
    )(x, merged, w_out, gate1, norm2_g, shift2, scale2)


def _peer_score_kernel(h_ref, wq_ref, k1_ref, k2_ref, s1_ref, s2_ref):
    q = jnp.dot(h_ref[...], wq_ref[...], preferred_element_type=F32).astype(BF16)
    nt = (((1,), (1,)), ((), ()))
    half = N_KEYS
    for h in range(HEADS):
        base = h * 2 * half
        s1_ref[h] = lax.dot_general(k1_ref[...], q[:, base:base + half], nt,
                                    preferred_element_type=F32)
        s2_ref[h] = lax.dot_general(k2_ref[...], q[:, base + half:base + 2 * half], nt,
                                    preferred_element_type=F32)


def _peer_scores(h2, wq, k1, k2, tm):
    t = h2.shape[0]
    out = jax.ShapeDtypeStruct((HEADS, N_KEYS, t), F32)
    ospec = pl.BlockSpec((HEADS, N_KEYS, tm), lambda i: (0, 0, i))
    kspec = pl.BlockSpec((N_KEYS, N_KEYS), lambda i: (0, 0))
    return pl.pallas_call(
        _peer_score_kernel,
        grid=(t // tm,),
        in_specs=[pl.BlockSpec((tm, D_MODEL), lambda i: (i, 0)),
                  pl.BlockSpec((D_MODEL, D_MODEL), lambda i: (0, 0)),
                  kspec, kspec],
        out_specs=[ospec, ospec],
        out_shape=[out, out],
        compiler_params=_cparams(("parallel",)),
        name="peer_scores",
    )(h2, wq, k1, k2)


_N_RANK = PEER_TOPK + 1
_CAND_PAIRS = [(i, j) for i in range(_N_RANK) for j in range(_N_RANK) if (i + 1) * (j + 1) <= _N_RANK]
_CAND_ROWS = -(-len(_CAND_PAIRS) // 8) * 8


def _top_values(x, n):
    vals = []
    for _ in range(n):
        mx = jnp.max(x, axis=0, keepdims=True)
        vals.append(mx)
        x = jnp.where(x == mx, -jnp.inf, x)
    return vals


def _peer_topk_kernel(s1_ref, s2_ref, c1_ref, d1_ref, e2_ref, cand_sc):
    for h in range(HEADS):
        a = s1_ref[h]
        b = s2_ref[h]
        v1 = _top_values(a, _N_RANK)
        v2 = _top_values(b, _N_RANK)
        cand_sc[...] = jnp.full_like(cand_sc, -jnp.inf)
        for r, (i, j) in enumerate(_CAND_PAIRS):
            cand_sc[r:r + 1, :] = v1[i] + v2[j]
        tops = _top_values(cand_sc[...], _N_RANK)
        tau = 0.5 * (tops[PEER_TOPK - 1] + tops[PEER_TOPK])
        z = jnp.exp(tops[0] - tops[0])
        for t in tops[1:PEER_TOPK]:
            z = z + jnp.exp(t - tops[0])
        c1_ref[h] = jnp.exp(a - v1[0]) / z
        d1_ref[h] = tau - a
        e2_ref[h] = jnp.exp(b - v2[0])


def _peer_topk(s1t, s2t, tl):
    t = s1t.shape[2]
    spec = pl.BlockSpec((HEADS, N_KEYS, tl), lambda i: (0, 0, i))
    out = jax.ShapeDtypeStruct(s1t.shape, F32)
    return pl.pallas_call(
        _peer_topk_kernel,
        grid=(t // tl,),
        in_specs=[spec, spec],
        out_specs=[spec, spec, spec],
        out_shape=[out, out, out],
        scratch_shapes=[pltpu.VMEM((_CAND_ROWS, tl), F32)],
        compiler_params=_cparams(("parallel",)),
        name="peer_topk",
    )(s1t, s2t)


def _peer_dense_kernel(h_ref, u_ref, vt_ref, c1_ref, d1_ref, s2_ref, e2_ref, o_ref,
                       a_sc, w_sc, acc_sc):
    c = pl.program_id(1)

    @pl.when(c == 0)
    def _():
        acc_sc[...] = jnp.zeros_like(acc_sc)

    a_sc[...] = lax.dot_general(u_ref[...], h_ref[...], (((1,), (1,)), ((), ())),
                                preferred_element_type=F32)
    for ii in range(EXPERT_CHUNK // N_KEYS):
        rows = slice(ii * N_KEYS, (ii + 1) * N_KEYS)
        a = a_sc[rows, :]
        g = jnp.zeros_like(a)
        for h in range(HEADS):
            d = d1_ref[h, ii:ii + 1, :]
            cf = c1_ref[h, ii:ii + 1, :]
            g = g + jnp.where(s2_ref[h] >= d, e2_ref[h] * cf, 0.0)
        gelu = a * (lax.erf(a / math.sqrt(2.0)) + 1.0) / 2.0
        w_sc[rows, :] = (g * gelu).astype(w_sc.dtype)
    acc_sc[...] += jnp.dot(vt_ref[...], w_sc[...], preferred_element_type=F32)

    @pl.when(c == pl.num_programs(1) - 1)
    def _():
        o_ref[...] = acc_sc[...].T


def _peer_dense(h2, u_bf, vt_bf, c1t, d1t, s2t, e2t, tb):
    t = h2.shape[0]
    n_exp = u_bf.shape[0]
    rows = EXPERT_CHUNK // N_KEYS
    small = pl.BlockSpec((HEADS, rows, tb), lambda i, c: (0, c, i))
    big = pl.BlockSpec((HEADS, N_KEYS, tb), lambda i, c: (0, 0, i))
    return pl.pallas_call(
        _peer_dense_kernel,
        grid=(t // tb, n_exp // EXPERT_CHUNK),
        in_specs=[pl.BlockSpec((tb, D_MODEL), lambda i, c: (i, 0)),
                  pl.BlockSpec((EXPERT_CHUNK, D_MODEL), lambda i, c: (c, 0)),
                  pl.BlockSpec((D_MODEL, EXPERT_CHUNK), lambda i, c: (0, c)),
                  small, small, big, big],
        out_specs=pl.BlockSpec((tb, D_MODEL), lambda i, c: (i, 0)),
        out_shape=jax.ShapeDtypeStruct((t, D_MODEL), F32),
        scratch_shapes=[pltpu.VMEM((EXPERT_CHUNK, tb), F32),
                        pltpu.VMEM((EXPERT_CHUNK, tb), BF16),
                        pltpu.VMEM((D_MODEL, tb), F32)],
        compiler_params=_cparams(("parallel", "arbitrary")),
        name="peer_dense",
    )(h2, u_bf, vt_bf, c1t, d1t, s2t, e2t)


def _final_kernel(x_ref, p_ref, g2_ref, fg_ref, o_ref):
    x = x_ref[...] + _mod_rows(g2_ref) * p_ref[...]
    o_ref[...] = x * lax.rsqrt(jnp.mean(x * x, axis=-1, keepdims=True) + EPS) * fg_ref[...]


def _final(x1, peer, gate2, final_g, seq, tm):
    t = x1.shape[0]
    row = pl.BlockSpec((tm, D_MODEL), lambda i: (i, 0))
    return pl.pallas_call(
        _final_kernel,
        grid=(t // tm,),
        in_specs=[row, row, _mod_spec(gate2, tm, seq), pl.BlockSpec((1, D_MODEL), lambda i: (0, 0))],
        out_specs=row,
        out_shape=jax.ShapeDtypeStruct((t, D_MODEL), F32),
        compiler_params=_cparams(("parallel",)),
        name="final",
    )(x1, peer, gate2, final_g)


def _rope_tables(pos):
    inv = 1.0 / (ROPE_THETA ** (jnp.arange(0, QK_ROPE, 2, dtype=F32) / QK_ROPE))
    ang = pos.astype(F32)[:, None] * inv[None, :]
    cos, sin = jnp.cos(ang), jnp.sin(ang)
    z = jnp.zeros_like(cos)
    return (jnp.concatenate([cos, cos, z, z], axis=1),
            jnp.concatenate([-sin, z, z, z], axis=1),
            jnp.concatenate([z, sin, z, z], axis=1))


def _prep_weights(w_in, fox_fbias, w_ukv, w_o_mla, w_o_fox, w_out, peer_wq, peer_keys1, peer_keys2,
                  peer_u, peer_v):
    d = D_MODEL
    n_q = HEADS * (HEAD_DIM + QK_ROPE)
    wq = w_in[:, :n_q].reshape(d, HEADS, HEAD_DIM + QK_ROPE)
    wq_cat = jnp.concatenate([wq, jnp.zeros((d, HEADS, QK_ROPE), F32)], axis=-1).reshape(d, HEADS * 2 * LANE)
    o1 = n_q + KV_LORA
    o2 = o1 + QK_ROPE
    o3 = o2 + 3 * HEADS * HEAD_DIM
    o4 = o3 + HEADS
    w_kv = jnp.concatenate([w_in[:, n_q:o2], jnp.zeros((d, LANE - QK_ROPE), F32),
                            w_in[:, o3:o4], jnp.zeros((d, LANE - HEADS), F32)], axis=1)
    fb = jnp.concatenate([fox_fbias, jnp.zeros((LANE - HEADS,), F32)])[None, :]
    w_ukv3 = w_ukv.reshape(KV_LORA, HEADS, 2 * HEAD_DIM)
    w_ukv_r = jnp.concatenate([w_ukv3[:, :, :HEAD_DIM].reshape(KV_LORA, -1),
                               w_ukv3[:, :, HEAD_DIM:].reshape(KV_LORA, -1)], axis=1)
    return dict(
        wq_cat=wq_cat.astype(BF16), w_kv=w_kv.astype(BF16), fb=fb,
        w_fox=w_in[:, o2:o3].astype(BF16), w_gate=w_in[:, o4:].astype(BF16),
        w_ukv=w_ukv_r.astype(BF16), w_o_mla=w_o_mla.astype(BF16), w_o_fox=w_o_fox.astype(BF16),
        w_out=w_out.astype(BF16), peer_wq=peer_wq.astype(BF16),
        k1=peer_keys1.astype(BF16), k2=peer_keys2.astype(BF16),
        u=peer_u.astype(BF16), vt=peer_v.T.astype(BF16))


def _layer(x, mods, caches, wts, norm1_g, kv_norm_g, norm2_g, final_g, *, batch, seq, past, tm, tq, tk):
    t = batch * seq
    shift1, scale1, gate1, shift2, scale2, gate2 = mods
    width = HEADS * HEAD_DIM
    pos = past + jnp.arange(seq, dtype=jnp.int32)
    tabs = _rope_tables(pos)
    if tm > seq:
        tabs = tuple(jnp.tile(tb, (tm // seq, 1)) for tb in tabs)

    h = _norm_mod(x, norm1_g, shift1, scale1, seq, min(tm, 512))
    q_mla = _proj_qmla(h, wts["wq_cat"], tabs, tm, LOG2E * (HEAD_DIM + QK_ROPE) ** -0.5)
    lat, lat_b, k_r, k_rp, logf = _proj_kv(h, wts["w_kv"], kv_norm_g, wts["fb"], tabs, tm)
    w_fox = wts["w_fox"]
    (q_fox,) = _proj(h, w_fox[:, :width], [BF16], tm, 512, scale=LOG2E * HEAD_DIM ** -0.5,
                      name="proj_foxq")
    k_new, k_new_b = _proj(h, w_fox[:, width:2 * width], [F32, BF16], tm, 512, name="proj_foxk")
    v_new, v_new_b = _proj(h, w_fox[:, 2 * width:], [F32, BF16], tm, 512, name="proj_foxv")
    (gates,) = _proj(h, wts["w_gate"], [BF16], tm, 512, act="sigmoid", name="proj_gate")

    kv_len = past + seq
    l_pad = -(-kv_len // tk) * tk

    def with_cache(cache, new, cols):
        new = new.reshape(batch, seq, cols)
        if cache is None:
            return new
        parts = [cache.reshape(batch, past, cols).astype(new.dtype), new]
        if l_pad > kv_len:
            parts.append(jnp.zeros((batch, l_pad - kv_len, cols), new.dtype))
        return jnp.concatenate(parts, axis=1)

    c_lat, c_kr, c_k, c_v, c_logf = caches if caches is not None else (None,) * 5
    lat_all = with_cache(c_lat, lat_b, KV_LORA)
    kr_all = with_cache(None if c_kr is None else jnp.pad(c_kr, ((0, 0), (0, 0), (0, LANE - QK_ROPE))),
                        k_rp, LANE)
    k_all = with_cache(c_k, k_new_b, width)
    v_all = with_cache(c_v, v_new_b, width)
    logf_all = with_cache(c_logf, logf, HEADS)

    (kv_up,) = _proj(lat_all.reshape(batch * l_pad, KV_LORA), wts["w_ukv"], [BF16],
                     min(1024, batch * l_pad), 512, name="kv_up")
    kv_up = kv_up.reshape(batch, l_pad, 2 * width)
    o_mla = _attention(q_mla.reshape(batch, seq, -1), kv_up, kv_up, (kr_all,), fox=False,
                       tq=tq, tk=tk, past=past, kv_len=kv_len, v_col=1)

    q_ext, k_ext = _cumsum_ext(logf_all, tk)
    o_fox = _attention(q_fox.reshape(batch, seq, width), k_all, v_all, (q_ext, k_ext), fox=True,
                       tq=tq, tk=tk, past=past, kv_len=kv_len)

    merged = _merge(o_mla.reshape(t, width), o_fox.reshape(t, width), wts["w_o_mla"], wts["w_o_fox"],
                    gates, tm, 512)
    tm2 = 512 if gate1.ndim == 3 else 256
    x1, h2 = _outproj(x, merged, wts["w_out"], gate1, norm2_g, shift2, scale2, seq, tm2)

    s1t, s2t = _peer_scores(h2, wts["peer_wq"], wts["k1"], wts["k2"], tm2)
    c1t, d1t, e2t = _peer_topk(s1t, s2t, 256)
    peer = _peer_dense(h2, wts["u"], wts["vt"], c1t, d1t, s2t, e2t, 512)
    y = _final(x1, peer, gate2, final_g, seq, tm2)
    return y, (lat, k_r, k_new, v_new, logf)


def kernel(x_prompt, x_sample, c_prompt, c_sample, cache_mla_latent, cache_mla_krope, cache_fox_k,
           cache_fox_v, cache_fox_logf, w_ada, b_ada, norm1_g, w_in, fox_fbias, kv_norm_g, w_ukv,
           w_o_mla, w_o_fox, w_out, norm2_g, peer_wq, peer_keys1, peer_keys2, peer_u, peer_v, final_g):
    depth = w_ada.shape[0]
    assert depth == 1, "the final RMSNorm is fused into the single trunk layer"
    bp, sp, d = x_prompt.shape
    bs, ss, _ = x_sample.shape
    past = cache_mla_latent.shape[2]
    xp = x_prompt.reshape(bp * sp, d)
    xs = x_sample.reshape(bs * ss, d)
    c_rows = -(-(bp + bs) // 16) * 16
    c_all = jnp.concatenate([c_prompt, c_sample, jnp.zeros((c_rows - bp - bs, d), F32)], axis=0)
    fg = final_g[None, :]
    states_p, states_s = [], []
    for l in range(depth):
        wts = _prep_weights(w_in[l], fox_fbias[l], w_ukv[l], w_o_mla[l], w_o_fox[l], w_out[l],
                            peer_wq[l], peer_keys1[l], peer_keys2[l], peer_u[l], peer_v[l])
        mod = _ada(c_all, w_ada[l], b_ada[l][None, :])
        mods_p = tuple(m[:, None, :] for m in jnp.split(mod[:bp], N_MOD, axis=-1))
        mods_s = tuple(jnp.repeat(m, ss, axis=0) for m in jnp.split(mod[bp:bp + bs], N_MOD, axis=-1))
        n1, kvg, n2 = norm1_g[l][None, :], kv_norm_g[l][None, :], norm2_g[l][None, :]
        xp, st_p = _layer(xp, mods_p, None, wts, n1, kvg, n2, fg,
                          batch=bp, seq=sp, past=0, tm=1024, tq=512, tk=512)
        caches = (cache_mla_latent[l], cache_mla_krope[l], cache_fox_k[l], cache_fox_v[l], cache_fox_logf[l])
        xs, st_s = _layer(xs, mods_s, caches, wts, n1, kvg, n2, fg,
                          batch=bs, seq=ss, past=past, tm=1024, tq=ss, tk=1152)
        states_p.append(st_p)
        states_s.append(st_s)

    def stack(states, i, batch, seq, tail):
        return jnp.stack([st[i].reshape((batch, seq) + tail) for st in states])

    tails = [(KV_LORA,), (QK_ROPE,), (HEADS, HEAD_DIM), (HEADS, HEAD_DIM), (HEADS,)]
    outs_p = [stack(states_p, i, bp, sp, tails[i]) for i in range(5)]
    outs_s = [stack(states_s, i, bs, ss, tails[i]) for i in range(5)]
    return (xp.reshape(bp, sp, d), xs.reshape(bs, ss, d), *outs_p, *outs_s)
```

```python
import functools
import math

import jax
import jax.numpy as jnp
import numpy as np
from jax import lax
from jax.experimental import pallas as pl
from jax.experimental.pallas import tpu as pltpu

F32 = jnp.float32
BF16 = jnp.bfloat16

D_MODEL = 2048
EPS = 1e-6
NEG_INF = -1e30
LOG2E = 1.4426950408889634
N_MOD = 6
HEADS = 8
HEAD_DIM = 128
QK_ROPE = 64
KV_LORA = 512
CHUNK_LOG2 = 6
ROPE_THETA = 10000.0
N_KEYS = 128
PEER_TOPK = 16
LANE = 128
EXT_COLS = 8
EXPERT_CHUNK = 1024
VMEM_LIMIT_MB = 56


def _cparams(sem, vmem_mb=VMEM_LIMIT_MB, flags=None):
    return pltpu.CompilerParams(dimension_semantics=sem, vmem_limit_bytes=vmem_mb * 1024 * 1024,
                                flags=flags)


def _mod_rows(ref):
    v = ref[...]
    return v[0] if v.ndim == 3 else v


def _mod_spec(mod, tm, seq):
    if mod.ndim == 3:
        blocks_per_batch = seq // tm
        return pl.BlockSpec((1, 1, D_MODEL), lambda i, *_: (i // blocks_per_batch, 0, 0))
    return pl.BlockSpec((tm, D_MODEL), lambda i, *_: (i, 0))


def _rope128(r, c, t1, t2):
    return r * c + pltpu.roll(r, 96, 1) * t1 + pltpu.roll(r, 32, 1) * t2


def _ada_kernel(c_ref, w_ref, b_ref, o_ref):
    c = c_ref[...]
    a = (c * jax.nn.sigmoid(c)).astype(BF16)
    o_ref[...] = jnp.dot(a, w_ref[...].astype(BF16), preferred_element_type=F32) + b_ref[...]


def _ada(c_all, w_ada, b_ada):
    rows, n = c_all.shape[0], w_ada.shape[1]
    tn = 1024
    return pl.pallas_call(
        _ada_kernel,
        grid=(n // tn,),
        in_specs=[pl.BlockSpec((rows, D_MODEL), lambda j: (0, 0)),
                  pl.BlockSpec((D_MODEL, tn), lambda j: (0, j)),
                  pl.BlockSpec((1, tn), lambda j: (0, j))],
        out_specs=pl.BlockSpec((rows, tn), lambda j: (0, j)),
        out_shape=jax.ShapeDtypeStruct((rows, n), F32),
        compiler_params=_cparams(("parallel",)),
        name="ada",
    )(c_all, w_ada, b_ada)


def _norm_mod_kernel(x_ref, g_ref, sh_ref, sc_ref, o_ref):
    x = x_ref[...]
    y = x * lax.rsqrt(jnp.mean(x * x, axis=-1, keepdims=True) + EPS) * g_ref[...]
    o_ref[...] = (y * (1.0 + _mod_rows(sc_ref)) + _mod_rows(sh_ref)).astype(o_ref.dtype)


def _norm_mod(x, g, shift, scale, seq, tm):
    t = x.shape[0]
    return pl.pallas_call(
        _norm_mod_kernel,
        grid=(t // tm,),
        in_specs=[pl.BlockSpec((tm, D_MODEL), lambda i: (i, 0)),
                  pl.BlockSpec((1, D_MODEL), lambda i: (0, 0)),
                  _mod_spec(shift, tm, seq), _mod_spec(scale, tm, seq)],
        out_specs=pl.BlockSpec((tm, D_MODEL), lambda i: (i, 0)),
        out_shape=jax.ShapeDtypeStruct((t, D_MODEL), BF16),
        compiler_params=_cparams(("parallel",)),
        name="norm_mod",
    )(x, g, shift, scale)


def _proj_kernel(h_ref, w_ref, *o_refs, scale, act):
    acc = jnp.dot(h_ref[...], w_ref[...], preferred_element_type=F32)
    if scale != 1.0:
        acc = acc * scale
    if act == "sigmoid":
        acc = jax.nn.sigmoid(acc)
    for o_ref in o_refs:
        o_ref[...] = acc.astype(o_ref.dtype)


def _proj(h, w, out_dtypes, tm, tn, scale=1.0, act=None, name="proj"):
    t, k = h.shape
    n = w.shape[1]
    outs = pl.pallas_call(
        functools.partial(_proj_kernel, scale=scale, act=act),
        grid=(t // tm, n // tn),
        in_specs=[pl.BlockSpec((tm, k), lambda i, j: (i, 0)),
                  pl.BlockSpec((k, tn), lambda i, j: (0, j))],
        out_specs=[pl.BlockSpec((tm, tn), lambda i, j: (i, j)) for _ in out_dtypes],
        out_shape=[jax.ShapeDtypeStruct((t, n), dt) for dt in out_dtypes],
        compiler_params=_cparams(("parallel", "parallel")),
        name=name,
    )(h, w)
    return outs


def _proj_qmla_kernel(h_ref, w_ref, c_ref, t1_ref, t2_ref, o_ref, *, scale):
    acc = jnp.dot(h_ref[...], w_ref[...], preferred_element_type=F32) * scale
    for hh in range(acc.shape[1] // (2 * LANE)):
        base = hh * 2 * LANE
        rot = _rope128(acc[:, base + LANE:base + 2 * LANE], c_ref[...], t1_ref[...], t2_ref[...])
        o_ref[:, base:base + LANE] = acc[:, base:base + LANE].astype(o_ref.dtype)
        o_ref[:, base + LANE:base + 2 * LANE] = rot.astype(o_ref.dtype)


def _proj_qmla(h, w, tabs, tm, scale, heads_per_step=4):
    t = h.shape[0]
    ntab = tabs[0].shape[0] // tm
    tn = heads_per_step * 2 * LANE
    tab_spec = pl.BlockSpec((tm, LANE), lambda i, j: (i % ntab, 0))
    return pl.pallas_call(
        functools.partial(_proj_qmla_kernel, scale=scale),
        grid=(t // tm, HEADS // heads_per_step),
        in_specs=[pl.BlockSpec((tm, D_MODEL), lambda i, j: (i, 0)),
                  pl.BlockSpec((D_MODEL, tn), lambda i, j: (0, j)),
                  tab_spec, tab_spec, tab_spec],
        out_specs=pl.BlockSpec((tm, tn), lambda i, j: (i, j)),
        out_shape=jax.ShapeDtypeStruct((t, HEADS * 2 * LANE), BF16),
        compiler_params=_cparams(("parallel", "parallel")),
        name="proj_qmla",
    )(h, w, *tabs)


def _proj_kv_kernel(h_ref, w_ref, g_ref, fb_ref, c_ref, t1_ref, t2_ref,
                    lat_ref, latb_ref, kr_ref, krp_ref, logf_ref):
    acc = jnp.dot(h_ref[...], w_ref[...], preferred_element_type=F32)
    ckv = acc[:, :KV_LORA]
    lat = ckv * lax.rsqrt(jnp.mean(ckv * ckv, axis=-1, keepdims=True) + EPS) * g_ref[...]
    lat_ref[...] = lat
    latb_ref[...] = lat.astype(latb_ref.dtype)
    rot = _rope128(acc[:, KV_LORA:KV_LORA + LANE], c_ref[...], t1_ref[...], t2_ref[...])
    kr_ref[...] = rot[:, :QK_ROPE]
    krp_ref[...] = rot.astype(krp_ref.dtype)
    f = acc[:, KV_LORA + LANE:] + fb_ref[...]
    logf = jnp.minimum(f, 0.0) - jnp.log1p(jnp.exp(-jnp.abs(f)))
    logf_ref[...] = logf[:, :HEADS]


def _proj_kv(h, w, g, fb, tabs, tm):
    t = h.shape[0]
    n = w.shape[1]
    ntab = tabs[0].shape[0] // tm
    tab_spec = pl.BlockSpec((tm, LANE), lambda i: (i % ntab, 0))
    row = lambda width: pl.BlockSpec((tm, width), lambda i: (i, 0))
    return pl.pallas_call(
        _proj_kv_kernel,
        grid=(t // tm,),
        in_specs=[row(D_MODEL),
                  pl.BlockSpec((D_MODEL, n), lambda i: (0, 0)),
                  pl.BlockSpec((1, KV_LORA), lambda i: (0, 0)),
                  pl.BlockSpec((1, LANE), lambda i: (0, 0)),
                  tab_spec, tab_spec, tab_spec],
        out_specs=[row(KV_LORA), row(KV_LORA), row(QK_ROPE), row(LANE), row(HEADS)],
        out_shape=[jax.ShapeDtypeStruct((t, KV_LORA), F32),
                   jax.ShapeDtypeStruct((t, KV_LORA), BF16),
                   jax.ShapeDtypeStruct((t, QK_ROPE), F32),
                   jax.ShapeDtypeStruct((t, LANE), BF16),
                   jax.ShapeDtypeStruct((t, HEADS), F32)],
        compiler_params=_cparams(("parallel",)),
        name="proj_kv",
    )(h, w, g, fb, *tabs)


def _split3(x):
    hi = x.astype(BF16).astype(F32)
    r1 = x - hi
    mid = r1.astype(BF16).astype(F32)
    lo = (r1 - mid).astype(BF16).astype(F32)
    return hi, mid, lo


def _cumsum_kernel(x_ref, pq_ref, pk_ref, oq_ref, ok_ref, qa_ref, ka_ref, carry_ref, *, blk):
    @pl.when(pl.program_id(1) == 0)
    def _():
        carry_ref[...] = jnp.zeros_like(carry_ref)

    hi, mid, lo = _split3(x_ref[0])
    rows = lax.broadcasted_iota(jnp.int32, (blk, blk), 0)
    cols = lax.broadcasted_iota(jnp.int32, (blk, blk), 1)
    tril = jnp.where(cols <= rows, 1.0, 0.0).astype(BF16)
    y = (jnp.dot(tril, lo.astype(BF16), preferred_element_type=F32)
         + jnp.dot(tril, mid.astype(BF16), preferred_element_type=F32)
         + jnp.dot(tril, hi.astype(BF16), preferred_element_type=F32)) + carry_ref[...]
    carry_ref[...] = y[blk - 1:blk, :]
    pieces = _split3(y * LOG2E)
    qa = oq_ref[...]
    ka = ok_ref[...]
    for j, piece in enumerate(pieces):
        qa = qa + jnp.dot(piece, pq_ref[j], preferred_element_type=F32)
        ka = ka - jnp.dot(piece, pk_ref[j], preferred_element_type=F32)
    qa_ref[0] = qa.astype(qa_ref.dtype)
    ka_ref[0] = ka.astype(ka_ref.dtype)


def _ext_placement():
    pq = np.zeros((3, HEADS, LANE), np.float32)
    pk = np.zeros((3, HEADS, LANE), np.float32)
    oq = np.zeros((1, LANE), np.float32)
    ok = np.zeros((1, LANE), np.float32)
    for h in range(HEADS):
        for j in range(3):
            pq[j, h, h * EXT_COLS + j] = 1.0
            pk[j, h, h * EXT_COLS + 3 + j] = 1.0
            oq[0, h * EXT_COLS + 3 + j] = 1.0
            ok[0, h * EXT_COLS + j] = 1.0
    return jnp.asarray(pq), jnp.asarray(pk), jnp.asarray(oq), jnp.asarray(ok)


def _cumsum_ext(logf, blk):
    b, l, h = logf.shape
    width = LANE
    pq, pk, oq, ok = _ext_placement()
    pspec = pl.BlockSpec((3, HEADS, width), lambda bi, i: (0, 0, 0))
    ospec = pl.BlockSpec((1, width), lambda bi, i: (0, 0))
    out = jax.ShapeDtypeStruct((b, l, width), BF16)
    return pl.pallas_call(
        functools.partial(_cumsum_kernel, blk=blk),
        grid=(b, l // blk),
        in_specs=[pl.BlockSpec((1, blk, h), lambda bi, i: (bi, i, 0)), pspec, pspec, ospec, ospec],
        out_specs=[pl.BlockSpec((1, blk, width), lambda bi, i: (bi, i, 0))] * 2,
        out_shape=[out, out],
        scratch_shapes=[pltpu.VMEM((1, h), F32)],
        compiler_params=_cparams(("parallel", "arbitrary")),
        name="cumsum",
    )(logf, pq, pk, oq, ok)


def _attn_kernel(*refs, fox, tq, tk, nk, past, chunk_log2, kv_len, l_pad):
    if fox:
        q_ref, qx_ref, k_ref, kx_ref, v_ref, o_ref, m_sc, acc_sc = refs
    else:
        q_ref, k_ref, kx_ref, v_ref, o_ref, m_sc, acc_sc = refs
    qi = pl.program_id(1)
    ki = pl.program_id(2)
    q_lo = past + qi * tq
    q_hi = q_lo + (tq - 1)
    k_lo = ki * tk
    k_hi = k_lo + (tk - 1)
    if nk == 1:
        last_k = 0
    else:
        last_k = jnp.minimum(nk - 1, ((((q_hi >> chunk_log2) + 1) << chunk_log2) - 1) // tk)
    needed = ki <= last_k
    full = jnp.logical_and((k_hi >> chunk_log2) <= (q_lo >> chunk_log2), k_hi < kv_len)

    @pl.when(ki == 0)
    def _init():
        m_sc[...] = jnp.full_like(m_sc, NEG_INF)
        acc_sc[...] = jnp.zeros_like(acc_sc)

    def step(masked):
        if masked:
            rows = q_lo + lax.broadcasted_iota(jnp.int32, (tq, tk), 0)
            cols = k_lo + lax.broadcasted_iota(jnp.int32, (tq, tk), 1)
            ok = (cols >> chunk_log2) <= (rows >> chunk_log2)
            if l_pad != kv_len:
                ok = jnp.logical_and(ok, cols < kv_len)
        ext_lane = lax.broadcasted_iota(jnp.int32, (tk, LANE), 1)
        ones_col = jnp.where(ext_lane == 0, 1.0, 0.0).astype(BF16)
        for h in range(HEADS):
            hs = slice(h * HEAD_DIM, (h + 1) * HEAD_DIM)
            if fox:
                q = jnp.concatenate([q_ref[0, :, hs], qx_ref[0]], axis=-1)
                kx = kx_ref[0]
                kx = jnp.where(ext_lane // EXT_COLS == h, kx, jnp.zeros_like(kx))
                k = jnp.concatenate([k_ref[0, :, hs], kx], axis=-1)
            else:
                q = q_ref[0, :, 2 * h * LANE:2 * (h + 1) * LANE]
                k = jnp.concatenate([k_ref[0, :, hs], kx_ref[0]], axis=-1)
            s = lax.dot_general(q, k, (((1,), (1,)), ((), ())), preferred_element_type=F32)
            if masked:
                s = jnp.where(ok, s, NEG_INF)
            m_prev = m_sc[h]
            m_new = jnp.maximum(m_prev, jnp.max(s, axis=-1, keepdims=True))
            alpha = jnp.exp2(m_prev - m_new)
            p = jnp.exp2(s - jnp.tile(m_new, (1, tk // LANE)))
            v = jnp.concatenate([v_ref[0, :, hs], ones_col], axis=-1)
            acc_sc[h] = jnp.tile(alpha, (1, 2)) * acc_sc[h] + jnp.dot(
                p.astype(BF16), v, preferred_element_type=F32)
            m_sc[h] = m_new

    @pl.when(jnp.logical_and(needed, full))
    def _full():
        step(False)

    @pl.when(jnp.logical_and(needed, jnp.logical_not(full)))
    def _masked():
        step(True)

    @pl.when(ki == last_k)
    def _fin():
        for h in range(HEADS):
            hs = slice(h * HEAD_DIM, (h + 1) * HEAD_DIM)
            acc = acc_sc[h]
            o_ref[0, :, hs] = (acc[:, :HEAD_DIM] / acc[:, HEAD_DIM:HEAD_DIM + 1]).astype(o_ref.dtype)


def _attention(q, k, v, extra, *, fox, tq, tk, past, kv_len, v_col=0):
    b, sq, qcols = q.shape
    l_pad = k.shape[1]
    nq, nk = sq // tq, l_pad // tk
    chunk_log2 = 0 if fox else CHUNK_LOG2
    width = HEADS * HEAD_DIM

    def kmap(bi, qi, ki):
        if nk == 1:
            return 0
        q_hi = past + qi * tq + (tq - 1)
        last_k = jnp.minimum(nk - 1, ((((q_hi >> chunk_log2) + 1) << chunk_log2) - 1) // tk)
        return jnp.minimum(ki, last_k)

    q_spec = pl.BlockSpec((1, tq, qcols), lambda bi, qi, ki: (bi, qi, 0))
    k_spec = pl.BlockSpec((1, tk, width), lambda bi, qi, ki: (bi, kmap(bi, qi, ki), 0))
    v_spec = pl.BlockSpec((1, tk, width), lambda bi, qi, ki: (bi, kmap(bi, qi, ki), v_col))
    kx_spec = pl.BlockSpec((1, tk, LANE), lambda bi, qi, ki: (bi, kmap(bi, qi, ki), 0))
    if fox:
        q_blk_off = past // tq
        qx_spec = pl.BlockSpec((1, tq, LANE), lambda bi, qi, ki: (bi, q_blk_off + qi, 0))
        in_specs = [q_spec, qx_spec, k_spec, kx_spec, v_spec]
        operands = (q, extra[0], k, extra[1], v)
    else:
        in_specs = [q_spec, k_spec, kx_spec, v_spec]
        operands = (q, k, extra[0], v)
    return pl.pallas_call(
        functools.partial(_attn_kernel, fox=fox, tq=tq, tk=tk, nk=nk, past=past,
                          chunk_log2=chunk_log2, kv_len=kv_len, l_pad=l_pad),
        grid=(b, nq, nk),
        in_specs=in_specs,
        out_specs=pl.BlockSpec((1, tq, width), lambda bi, qi, ki: (bi, qi, 0)),
        out_shape=jax.ShapeDtypeStruct((b, sq, width), BF16),
        scratch_shapes=[pltpu.VMEM((HEADS, tq, LANE), F32),
                        pltpu.VMEM((HEADS, tq, 2 * LANE), F32)],
        compiler_params=_cparams(("parallel", "parallel", "arbitrary")),
        name="attn_fox" if fox else "attn_mla",
    )(*operands)


def _merge_kernel(om_ref, of_ref, wm_ref, wf_ref, gm_ref, gf_ref, o_ref):
    a = jnp.dot(om_ref[...], wm_ref[...], preferred_element_type=F32)
    b = jnp.dot(of_ref[...], wf_ref[...], preferred_element_type=F32)
    o_ref[...] = (gm_ref[...].astype(F32) * a + gf_ref[...].astype(F32) * b).astype(o_ref.dtype)


def _merge(o_mla, o_fox, w_o_mla, w_o_fox, gates, tm, tn):
    t, width = o_mla.shape
    ncol = D_MODEL // tn
    return pl.pallas_call(
        _merge_kernel,
        grid=(t // tm, ncol),
        in_specs=[pl.BlockSpec((tm, width), lambda i, j: (i, 0)),
                  pl.BlockSpec((tm, width), lambda i, j: (i, 0)),
                  pl.BlockSpec((width, tn), lambda i, j: (0, j)),
                  pl.BlockSpec((width, tn), lambda i, j: (0, j)),
                  pl.BlockSpec((tm, tn), lambda i, j: (i, j)),
                  pl.BlockSpec((tm, tn), lambda i, j: (i, ncol + j))],
        out_specs=pl.BlockSpec((tm, tn), lambda i, j: (i, j)),
        out_shape=jax.ShapeDtypeStruct((t, D_MODEL), BF16),
        compiler_params=_cparams(("parallel", "parallel")),
        name="merge",
    )(o_mla, o_fox, w_o_mla, w_o_fox, gates, gates)


def _outproj_kernel(x_ref, mg_ref, w_ref, g1_ref, n2_ref, sh_ref, sc_ref, x1_ref, h2_ref):
    y = jnp.dot(mg_ref[...], w_ref[...], preferred_element_type=F32)
    x1 = x_ref[...] + _mod_rows(g1_ref) * y
    x1_ref[...] = x1
    n = x1 * lax.rsqrt(jnp.mean(x1 * x1, axis=-1, keepdims=True) + EPS) * n2_ref[...]
    h2_ref[...] = (n * (1.0 + _mod_rows(sc_ref)) + _mod_rows(sh_ref)).astype(h2_ref.dtype)


def _outproj(x, merged, w_out, gate1, norm2_g, shift2, scale2, seq, tm):
    t = x.shape[0]
    row = pl.BlockSpec((tm, D_MODEL), lambda i: (i, 0))
    return pl.pallas_call(
        _outproj_kernel,
        grid=(t // tm,),
        in_specs=[row, row,
                  pl.BlockSpec((D_MODEL, D_MODEL), lambda i: (0, 0)),
                  _mod_spec(gate1, tm, seq),
                  pl.BlockSpec((1, D_MODEL), lambda i: (0, 0)),
                  _mod_spec(shift2, tm, seq), _mod_spec(scale2, tm, seq)],
        out_specs=[row, row],
        out_shape=[jax.ShapeDtypeStruct((t, D_MODEL), F32),
                   jax.ShapeDtypeStruct((t, D_MODEL), BF16)],
        compiler_params=_cparams(("parallel",)),
        name="outproj",
    )(x, merged, w_out, gate1, norm2_g, shift2, scale2)


def _peer_score_kernel(h_ref, wq_ref, k1_ref, k2_ref, s1_ref, s2_ref):
    q = jnp.dot(h_ref[...], wq_ref[...], preferred_element_type=F32).astype(BF16)
    nt = (((1,), (1,)), ((), ()))
    half = N_KEYS
    for h in range(HEADS):
        base = h * 2 * half
        s1_ref[h] = lax.dot_general(k1_ref[...], q[:, base:base + half], nt,
                                    preferred_element_type=F32)
        s2_ref[h] = lax.dot_general(k2_ref[...], q[:, base + half:base + 2 * half], nt,
                                    preferred_element_type=F32)


def _peer_scores(h2, wq, k1, k2, tm):
    t = h2.shape[0]
    out = jax.ShapeDtypeStruct((HEADS, N_KEYS, t), F32)
    ospec = pl.BlockSpec((HEADS, N_KEYS, tm), lambda i: (0, 0, i))
    kspec = pl.BlockSpec((N_KEYS, N_KEYS), lambda i: (0, 0))
    return pl.pallas_call(
        _peer_score_kernel,
        grid=(t // tm,),
        in_specs=[pl.BlockSpec((tm, D_MODEL), lambda i: (i, 0)),
                  pl.BlockSpec((D_MODEL, D_MODEL), lambda i: (0, 0)),
                  kspec, kspec],
        out_specs=[ospec, ospec],
        out_shape=[out, out],
        compiler_params=_cparams(("parallel",)),
        name="peer_scores",
    )(h2, wq, k1, k2)


_N_RANK = PEER_TOPK + 1
_CAND_PAIRS = [(i, j) for i in range(_N_RANK) for j in range(_N_RANK) if (i + 1) * (j + 1) <= _N_RANK]
_CAND_ROWS = -(-len(_CAND_PAIRS) // 8) * 8


def _top_values(x, n):
    vals = []
    for _ in range(n):
        mx = jnp.max(x, axis=0, keepdims=True)
        vals.append(mx)
        x = jnp.where(x == mx, -jnp.inf, x)
    return vals


def _peer_topk_kernel(s1_ref, s2_ref, c1_ref, d1_ref, e2_ref, cand_sc):
    for h in range(HEADS):
        a = s1_ref[h]
        b = s2_ref[h]
        v1 = _top_values(a, _N_RANK)
        v2 = _top_values(b, _N_RANK)
        cand_sc[...] = jnp.full_like(cand_sc, -jnp.inf)
        for r, (i, j) in enumerate(_CAND_PAIRS):
            cand_sc[r:r + 1, :] = v1[i] + v2[j]
        tops = _top_values(cand_sc[...], _N_RANK)
        tau = 0.5 * (tops[PEER_TOPK - 1] + tops[PEER_TOPK])
        z = jnp.exp(tops[0] - tops[0])
        for t in tops[1:PEER_TOPK]:
            z = z + jnp.exp(t - tops[0])
        c1_ref[h] = jnp.exp(a - v1[0]) / z
        d1_ref[h] = tau - a
        e2_ref[h] = jnp.exp(b - v2[0])


def _peer_topk(s1t, s2t, tl):
    t = s1t.shape[2]
    spec = pl.BlockSpec((HEADS, N_KEYS, tl), lambda i: (0, 0, i))
    out = jax.ShapeDtypeStruct(s1t.shape, F32)
    return pl.pallas_call(
        _peer_topk_kernel,
        grid=(t // tl,),
        in_specs=[spec, spec],
        out_specs=[spec, spec, spec],
        out_shape=[out, out, out],
        scratch_shapes=[pltpu.VMEM((_CAND_ROWS, tl), F32)],
        compiler_params=_cparams(("parallel",)),
        name="peer_topk",
    )(s1t, s2t)


def _peer_dense_kernel(h_ref, u_ref, vt_ref, c1_ref, d1_ref, s2_ref, e2_ref, o_ref,
                       a_sc, w_sc, acc_sc):
    c = pl.program_id(1)

    @pl.when(c == 0)
    def _():
        acc_sc[...] = jnp.zeros_like(acc_sc)

    a_sc[...] = lax.dot_general(u_ref[...], h_ref[...], (((1,), (1,)), ((), ())),
                                preferred_element_type=F32)
    for ii in range(EXPERT_CHUNK // N_KEYS):
        rows = slice(ii * N_KEYS, (ii + 1) * N_KEYS)
        a = a_sc[rows, :]
        g = jnp.zeros_like(a)
        for h in range(HEADS):
            d = d1_ref[h, ii:ii + 1, :]
            cf = c1_ref[h, ii:ii + 1, :]
            g = g + jnp.where(s2_ref[h] >= d, e2_ref[h] * cf, 0.0)
        gelu = a * (lax.erf(a / math.sqrt(2.0)) + 1.0) / 2.0
        w_sc[rows, :] = (g * gelu).astype(w_sc.dtype)
    acc_sc[...] += jnp.dot(vt_ref[...], w_sc[...], preferred_element_type=F32)

    @pl.when(c == pl.num_programs(1) - 1)
    def _():
        o_ref[...] = acc_sc[...].T


def _peer_dense(h2, u_bf, vt_bf, c1t, d1t, s2t, e2t, tb):
    t = h2.shape[0]
    n_exp = u_bf.shape[0]
    rows = EXPERT_CHUNK // N_KEYS
    small = pl.BlockSpec((HEADS, rows, tb), lambda i, c: (0, c, i))
    big = pl.BlockSpec((HEADS, N_KEYS, tb), lambda i, c: (0, 0, i))
    return pl.pallas_call(
        _peer_dense_kernel,
        grid=(t // tb, n_exp // EXPERT_CHUNK),
        in_specs=[pl.BlockSpec((tb, D_MODEL), lambda i, c: (i, 0)),
                  pl.BlockSpec((EXPERT_CHUNK, D_MODEL), lambda i, c: (c, 0)),
                  pl.BlockSpec((D_MODEL, EXPERT_CHUNK), lambda i, c: (0, c)),
                  small, small, big, big],
        out_specs=pl.BlockSpec((tb, D_MODEL), lambda i, c: (i, 0)),
        out_shape=jax.ShapeDtypeStruct((t, D_MODEL), F32),
        scratch_shapes=[pltpu.VMEM((EXPERT_CHUNK, tb), F32),
                        pltpu.VMEM((EXPERT_CHUNK, tb), BF16),
                        pltpu.VMEM((D_MODEL, tb), F32)],
        compiler_params=_cparams(("parallel", "arbitrary")),
        name="peer_dense",
    )(h2, u_bf, vt_bf, c1t, d1t, s2t, e2t)


def _final_kernel(x_ref, p_ref, g2_ref, fg_ref, o_ref):
    x = x_ref[...] + _mod_rows(g2_ref) * p_ref[...]
    o_ref[...] = x * lax.rsqrt(jnp.mean(x * x, axis=-1, keepdims=True) + EPS) * fg_ref[...]


def _final(x1, peer, gate2, final_g, seq, tm):
    t = x1.shape[0]
    row = pl.BlockSpec((tm, D_MODEL), lambda i: (i, 0))
    return pl.pallas_call(
        _final_kernel,
        grid=(t // tm,),
        in_specs=[row, row, _mod_spec(gate2, tm, seq), pl.BlockSpec((1, D_MODEL), lambda i: (0, 0))],
        out_specs=row,
        out_shape=jax.ShapeDtypeStruct((t, D_MODEL), F32),
        compiler_params=_cparams(("parallel",)),
        name="final",
    )(x1, peer, gate2, final_g)


def _rope_tables(pos):
    inv = 1.0 / (ROPE_THETA ** (jnp.arange(0, QK_ROPE, 2, dtype=F32) / QK_ROPE))
    ang = pos.astype(F32)[:, None] * inv[None, :]
    cos, sin = jnp.cos(ang), jnp.sin(ang)
    z = jnp.zeros_like(cos)
    return (jnp.concatenate([cos, cos, z, z], axis=1),
            jnp.concatenate([-sin, z, z, z], axis=1),
            jnp.concatenate([z, sin, z, z], axis=1))


def _prep_weights(w_in, fox_fbias, w_ukv, w_o_mla, w_o_fox, w_out, peer_wq, peer_keys1, peer_keys2,
                  peer_u, peer_v):
    d = D_MODEL
    n_q = HEADS * (HEAD_DIM + QK_ROPE)
    wq = w_in[:, :n_q].reshape(d, HEADS, HEAD_DIM + QK_ROPE)
    wq_cat = jnp.concatenate([wq, jnp.zeros((d, HEADS, QK_ROPE), F32)], axis=-1).reshape(d, HEADS * 2 * LANE)
    o1 = n_q + KV_LORA
    o2 = o1 + QK_ROPE
    o3 = o2 + 3 * HEADS * HEAD_DIM
    o4 = o3 + HEADS
    w_kv = jnp.concatenate([w_in[:, n_q:o2], jnp.zeros((d, LANE - QK_ROPE), F32),
                            w_in[:, o3:o4], jnp.zeros((d, LANE - HEADS), F32)], axis=1)
    fb = jnp.concatenate([fox_fbias, jnp.zeros((LANE - HEADS,), F32)])[None, :]
    w_ukv3 = w_ukv.reshape(KV_LORA, HEADS, 2 * HEAD_DIM)
    w_ukv_r = jnp.concatenate([w_ukv3[:, :, :HEAD_DIM].reshape(KV_LORA, -1),
                               w_ukv3[:, :, HEAD_DIM:].reshape(KV_LORA, -1)], axis=1)
    return dict(
        wq_cat=wq_cat.astype(BF16), w_kv=w_kv.astype(BF16), fb=fb,
        w_fox=w_in[:, o2:o3].astype(BF16), w_gate=w_in[:, o4:].astype(BF16),
        w_ukv=w_ukv_r.astype(BF16), w_o_mla=w_o_mla.astype(BF16), w_o_fox=w_o_fox.astype(BF16),
        w_out=w_out.astype(BF16), peer_wq=peer_wq.astype(BF16),
        k1=peer_keys1.astype(BF16), k2=peer_keys2.astype(BF16),
        u=peer_u.astype(BF16), vt=peer_v.T.astype(BF16))


def _layer(x, mods, caches, wts, norm1_g, kv_norm_g, norm2_g, final_g, *, batch, seq, past, tm, tq, tk):
    t = batch * seq
    shift1, scale1, gate1, shift2, scale2, gate2 = mods
    width = HEADS * HEAD_DIM
    pos = past + jnp.arange(seq, dtype=jnp.int32)
    tabs = _rope_tables(pos)
    if tm > seq:
        tabs = tuple(jnp.tile(tb, (tm // seq, 1)) for tb in tabs)

    h = _norm_mod(x, norm1_g, shift1, scale1, seq, min(tm, 512))
    q_mla = _proj_qmla(h, wts["wq_cat"], tabs, tm, LOG2E * (HEAD_DIM + QK_ROPE) ** -0.5)
    lat, lat_b, k_r, k_rp, logf = _proj_kv(h, wts["w_kv"], kv_norm_g, wts["fb"], tabs, tm)
    w_fox = wts["w_fox"]
    (q_fox,) = _proj(h, w_fox[:, :width], [BF16], tm, width, scale=LOG2E * HEAD_DIM ** -0.5,
                      name="proj_foxq")
    k_new, k_new_b = _proj(h, w_fox[:, width:2 * width], [F32, BF16], tm, width, name="proj_foxk")
    v_new, v_new_b = _proj(h, w_fox[:, 2 * width:], [F32, BF16], tm, width, name="proj_foxv")
    (gates,) = _proj(h, wts["w_gate"], [BF16], tm, 1024, act="sigmoid", name="proj_gate")

    kv_len = past + seq
    l_pad = -(-kv_len // tk) * tk

    def with_cache(cache, new, cols):
        new = new.reshape(batch, seq, cols)
        if cache is None:
            return new
        parts = [cache.reshape(batch, past, cols).astype(new.dtype), new]
        if l_pad > kv_len:
            parts.append(jnp.zeros((batch, l_pad - kv_len, cols), new.dtype))
        return jnp.concatenate(parts, axis=1)

    c_lat, c_kr, c_k, c_v, c_logf = caches if caches is not None else (None,) * 5
    lat_all = with_cache(c_lat, lat_b, KV_LORA)
    kr_all = with_cache(None if c_kr is None else jnp.pad(c_kr, ((0, 0), (0, 0), (0, LANE - QK_ROPE))),
                        k_rp, LANE)
    k_all = with_cache(c_k, k_new_b, width)
    v_all = with_cache(c_v, v_new_b, width)
    logf_all = with_cache(c_logf, logf, HEADS)

    (kv_up,) = _proj(lat_all.reshape(batch * l_pad, KV_LORA), wts["w_ukv"], [BF16],
                     min(1024, batch * l_pad), 2 * width, name="kv_up")
    kv_up = kv_up.reshape(batch, l_pad, 2 * width)
    o_mla = _attention(q_mla.reshape(batch, seq, -1), kv_up, kv_up, (kr_all,), fox=False,
                       tq=tq, tk=tk, past=past, kv_len=kv_len, v_col=1)

    q_ext, k_ext = _cumsum_ext(logf_all, tk)
    o_fox = _attention(q_fox.reshape(batch, seq, width), k_all, v_all, (q_ext, k_ext), fox=True,
                       tq=tq, tk=tk, past=past, kv_len=kv_len)

    merged = _merge(o_mla.reshape(t, width), o_fox.reshape(t, width), wts["w_o_mla"], wts["w_o_fox"],
                    gates, tm, 1024)
    tm2 = 512 if gate1.ndim == 3 else 256
    x1, h2 = _outproj(x, merged, wts["w_out"], gate1, norm2_g, shift2, scale2, seq, tm2)

    s1t, s2t = _peer_scores(h2, wts["peer_wq"], wts["k1"], wts["k2"], tm2)
    c1t, d1t, e2t = _peer_topk(s1t, s2t, 256)
    peer = _peer_dense(h2, wts["u"], wts["vt"], c1t, d1t, s2t, e2t, 512)
    y = _final(x1, peer, gate2, final_g, seq, tm2)
    return y, (lat, k_r, k_new, v_new, logf)


def kernel(x_prompt, x_sample, c_prompt, c_sample, cache_mla_latent, cache_mla_krope, cache_fox_k,
           cache_fox_v, cache_fox_logf, w_ada, b_ada, norm1_g, w_in, fox_fbias, kv_norm_g, w_ukv,
           w_o_mla, w_o_fox, w_out, norm2_g, peer_wq, peer_keys1, peer_keys2, peer_u, peer_v, final_g):
    depth = w_ada.shape[0]
    assert depth == 1, "the final RMSNorm is fused into the single trunk layer"
    bp, sp, d = x_prompt.shape
    bs, ss, _ = x_sample.shape
    past = cache_mla_latent.shape[2]
    xp = x_prompt.reshape(bp * sp, d)
    xs = x_sample.reshape(bs * ss, d)
    c_rows = -(-(bp + bs) // 16) * 16
    c_all = jnp.concatenate([c_prompt, c_sample, jnp.zeros((c_rows - bp - bs, d), F32)], axis=0)
    fg = final_g[None, :]
    states_p, states_s = [], []
    for l in range(depth):
        wts = _prep_weights(w_in[l], fox_fbias[l], w_ukv[l], w_o_mla[l], w_o_fox[l], w_out[l],
                            peer_wq[l], peer_keys1[l], peer_keys2[l], peer_u[l], peer_v[l])
        mod = _ada(c_all, w_ada[l], b_ada[l][None, :])
        mods_p = tuple(m[:, None, :] for m in jnp.split(mod[:bp], N_MOD, axis=-1))
        mods_s = tuple(jnp.repeat(m, ss, axis=0) for m in jnp.split(mod[bp:bp + bs], N_MOD, axis=-1))
        n1, kvg, n2 = norm1_g[l][None, :], kv_norm_g[l][None, :], norm2_g[l][None, :]
        xp, st_p = _layer(xp, mods_p, None, wts, n1, kvg, n2, fg,
                          batch=bp, seq=sp, past=0, tm=1024, tq=512, tk=512)
        caches = (cache_mla_latent[l], cache_mla_krope[l], cache_fox_k[l], cache_fox_v[l], cache_fox_logf[l])
        xs, st_s = _layer(xs, mods_s, caches, wts, n1, kvg, n2, fg,
                          batch=bs, seq=ss, past=past, tm=1024, tq=ss, tk=1152)
        states_p.append(st_p)
        states_s.append(st_s)

    def stack(states, i, batch, seq, tail):
        return jnp.stack([st[i].reshape((batch, seq) + tail) for st in states])

    tails = [(KV_LORA,), (QK_ROPE,), (HEADS, HEAD_DIM), (HEADS, HEAD_DIM), (HEADS,)]
    outs_p = [stack(states_p, i, bp, sp, tails[i]) for i in range(5)]
    outs_s = [stack(states_s, i, bs, ss, tails[i]) for i in range(5)]
    return (xp.reshape(bp, sp, d), xs.reshape(bs, ss, d), *outs_p, *outs_s)
```

```python
import functools
import math

import jax
import jax.numpy as jnp
import numpy as np
from jax import lax
from jax.experimental import pallas as pl
from jax.experimental.pallas import tpu as pltpu

F32 = jnp.float32
BF16 = jnp.bfloat16

D_MODEL = 2048
EPS = 1e-6
NEG_INF = -1e30
LOG2E = 1.4426950408889634
N_MOD = 6
HEADS = 8
HEAD_DIM = 128
QK_ROPE = 64
KV_LORA = 512
CHUNK_LOG2 = 6
ROPE_THETA = 10000.0
N_KEYS = 128
PEER_TOPK = 16
LANE = 128
EXT_COLS = 8
EXPERT_CHUNK = 1024
VMEM_LIMIT_MB = 56


def _cparams(sem, vmem_mb=VMEM_LIMIT_MB, flags=None):
    return pltpu.CompilerParams(dimension_semantics=sem, vmem_limit_bytes=vmem_mb * 1024 * 1024,
                                flags=flags)


def _mod_rows(ref):
    v = ref[...]
    return v[0] if v.ndim == 3 else v


def _mod_spec(mod, tm, seq):
    if mod.ndim == 3:
        blocks_per_batch = seq // tm
        return pl.BlockSpec((1, 1, D_MODEL), lambda i, *_: (i // blocks_per_batch, 0, 0))
    return pl.BlockSpec((tm, D_MODEL), lambda i, *_: (i, 0))


def _rope128(r, c, t1, t2):
    return r * c + pltpu.roll(r, 96, 1) * t1 + pltpu.roll(r, 32, 1) * t2


def _ada_kernel(c_ref, w_ref, b_ref, o_ref):
    c = c_ref[...]
    a = (c * jax.nn.sigmoid(c)).astype(BF16)
    o_ref[...] = jnp.dot(a, w_ref[...].astype(BF16), preferred_element_type=F32) + b_ref[...]


def _ada(c_all, w_ada, b_ada):
    rows, n = c_all.shape[0], w_ada.shape[1]
    tn = 1024
    return pl.pallas_call(
        _ada_kernel,
        grid=(n // tn,),
        in_specs=[pl.BlockSpec((rows, D_MODEL), lambda j: (0, 0)),
                  pl.BlockSpec((D_MODEL, tn), lambda j: (0, j)),
                  pl.BlockSpec((1, tn), lambda j: (0, j))],
        out_specs=pl.BlockSpec((rows, tn), lambda j: (0, j)),
        out_shape=jax.ShapeDtypeStruct((rows, n), F32),
        compiler_params=_cparams(("parallel",)),
        name="ada",
    )(c_all, w_ada, b_ada)


def _norm_mod_kernel(x_ref, g_ref, sh_ref, sc_ref, o_ref):
    x = x_ref[...]
    y = x * lax.rsqrt(jnp.mean(x * x, axis=-1, keepdims=True) + EPS) * g_ref[...]
    o_ref[...] = (y * (1.0 + _mod_rows(sc_ref)) + _mod_rows(sh_ref)).astype(o_ref.dtype)


def _norm_mod(x, g, shift, scale, seq, tm):
    t = x.shape[0]
    return pl.pallas_call(
        _norm_mod_kernel,
        grid=(t // tm,),
        in_specs=[pl.BlockSpec((tm, D_MODEL), lambda i: (i, 0)),
                  pl.BlockSpec((1, D_MODEL), lambda i: (0, 0)),
                  _mod_spec(shift, tm, seq), _mod_spec(scale, tm, seq)],
        out_specs=pl.BlockSpec((tm, D_MODEL), lambda i: (i, 0)),
        out_shape=jax.ShapeDtypeStruct((t, D_MODEL), BF16),
        compiler_params=_cparams(("parallel",)),
        name="norm_mod",
    )(x, g, shift, scale)


def _proj_kernel(h_ref, w_ref, *o_refs, scale, act):
    acc = jnp.dot(h_ref[...], w_ref[...], preferred_element_type=F32)
    if scale != 1.0:
        acc = acc * scale
    if act == "sigmoid":
        acc = jax.nn.sigmoid(acc)
    for o_ref in o_refs:
        o_ref[...] = acc.astype(o_ref.dtype)


def _proj(h, w, out_dtypes, tm, tn, scale=1.0, act=None, name="proj"):
    t, k = h.shape
    n = w.shape[1]
    outs = pl.pallas_call(
        functools.partial(_proj_kernel, scale=scale, act=act),
        grid=(t // tm, n // tn),
        in_specs=[pl.BlockSpec((tm, k), lambda i, j: (i, 0)),
                  pl.BlockSpec((k, tn), lambda i, j: (0, j))],
        out_specs=[pl.BlockSpec((tm, tn), lambda i, j: (i, j)) for _ in out_dtypes],
        out_shape=[jax.ShapeDtypeStruct((t, n), dt) for dt in out_dtypes],
        compiler_params=_cparams(("parallel", "parallel")),
        name=name,
    )(h, w)
    return outs


def _proj_qmla_kernel(h_ref, w_ref, c_ref, t1_ref, t2_ref, o_ref, *, scale):
    acc = jnp.dot(h_ref[...], w_ref[...], preferred_element_type=F32) * scale
    for hh in range(acc.shape[1] // (2 * LANE)):
        base = hh * 2 * LANE
        rot = _rope128(acc[:, base + LANE:base + 2 * LANE], c_ref[...], t1_ref[...], t2_ref[...])
        o_ref[:, base:base + LANE] = acc[:, base:base + LANE].astype(o_ref.dtype)
        o_ref[:, base + LANE:base + 2 * LANE] = rot.astype(o_ref.dtype)


def _proj_qmla(h, w, tabs, tm, scale, heads_per_step=4):
    t = h.shape[0]
    ntab = tabs[0].shape[0] // tm
    tn = heads_per_step * 2 * LANE
    tab_spec = pl.BlockSpec((tm, LANE), lambda i, j: (i % ntab, 0))
    return pl.pallas_call(
        functools.partial(_proj_qmla_kernel, scale=scale),
        grid=(t // tm, HEADS // heads_per_step),
        in_specs=[pl.BlockSpec((tm, D_MODEL), lambda i, j: (i, 0)),
                  pl.BlockSpec((D_MODEL, tn), lambda i, j: (0, j)),
                  tab_spec, tab_spec, tab_spec],
        out_specs=pl.BlockSpec((tm, tn), lambda i, j: (i, j)),
        out_shape=jax.ShapeDtypeStruct((t, HEADS * 2 * LANE), BF16),
        compiler_params=_cparams(("parallel", "parallel")),
        name="proj_qmla",
    )(h, w, *tabs)


def _proj_kv_kernel(h_ref, w_ref, g_ref, fb_ref, c_ref, t1_ref, t2_ref,
                    lat_ref, latb_ref, kr_ref, krp_ref, logf_ref):
    acc = jnp.dot(h_ref[...], w_ref[...], preferred_element_type=F32)
    ckv = acc[:, :KV_LORA]
    lat = ckv * lax.rsqrt(jnp.mean(ckv * ckv, axis=-1, keepdims=True) + EPS) * g_ref[...]
    lat_ref[...] = lat
    latb_ref[...] = lat.astype(latb_ref.dtype)
    rot = _rope128(acc[:, KV_LORA:KV_LORA + LANE], c_ref[...], t1_ref[...], t2_ref[...])
    kr_ref[...] = rot[:, :QK_ROPE]
    krp_ref[...] = rot.astype(krp_ref.dtype)
    f = acc[:, KV_LORA + LANE:] + fb_ref[...]
    logf = jnp.minimum(f, 0.0) - jnp.log1p(jnp.exp(-jnp.abs(f)))
    logf_ref[...] = logf[:, :HEADS]


def _proj_kv(h, w, g, fb, tabs, tm):
    t = h.shape[0]
    n = w.shape[1]
    ntab = tabs[0].shape[0] // tm
    tab_spec = pl.BlockSpec((tm, LANE), lambda i: (i % ntab, 0))
    row = lambda width: pl.BlockSpec((tm, width), lambda i: (i, 0))
    return pl.pallas_call(
        _proj_kv_kernel,
        grid=(t // tm,),
        in_specs=[row(D_MODEL),
                  pl.BlockSpec((D_MODEL, n), lambda i: (0, 0)),
                  pl.BlockSpec((1, KV_LORA), lambda i: (0, 0)),
                  pl.BlockSpec((1, LANE), lambda i: (0, 0)),
                  tab_spec, tab_spec, tab_spec],
        out_specs=[row(KV_LORA), row(KV_LORA), row(QK_ROPE), row(LANE), row(HEADS)],
        out_shape=[jax.ShapeDtypeStruct((t, KV_LORA), F32),
                   jax.ShapeDtypeStruct((t, KV_LORA), BF16),
                   jax.ShapeDtypeStruct((t, QK_ROPE), F32),
                   jax.ShapeDtypeStruct((t, LANE), BF16),
                   jax.ShapeDtypeStruct((t, HEADS), F32)],
        compiler_params=_cparams(("parallel",)),
        name="proj_kv",
    )(h, w, g, fb, *tabs)


def _split3(x):
    hi = x.astype(BF16).astype(F32)
    r1 = x - hi
    mid = r1.astype(BF16).astype(F32)
    lo = (r1 - mid).astype(BF16).astype(F32)
    return hi, mid, lo


def _cumsum_kernel(x_ref, pq_ref, pk_ref, oq_ref, ok_ref, qa_ref, ka_ref, carry_ref, *, blk):
    @pl.when(pl.program_id(1) == 0)
    def _():
        carry_ref[...] = jnp.zeros_like(carry_ref)

    hi, mid, lo = _split3(x_ref[0])
    rows = lax.broadcasted_iota(jnp.int32, (blk, blk), 0)
    cols = lax.broadcasted_iota(jnp.int32, (blk, blk), 1)
    tril = jnp.where(cols <= rows, 1.0, 0.0).astype(BF16)
    y = (jnp.dot(tril, lo.astype(BF16), preferred_element_type=F32)
         + jnp.dot(tril, mid.astype(BF16), preferred_element_type=F32)
         + jnp.dot(tril, hi.astype(BF16), preferred_element_type=F32)) + carry_ref[...]
    carry_ref[...] = y[blk - 1:blk, :]
    pieces = _split3(y * LOG2E)
    qa = oq_ref[...]
    ka = ok_ref[...]
    for j, piece in enumerate(pieces):
        qa = qa + jnp.dot(piece, pq_ref[j], preferred_element_type=F32)
        ka = ka - jnp.dot(piece, pk_ref[j], preferred_element_type=F32)
    qa_ref[0] = qa.astype(qa_ref.dtype)
    ka_ref[0] = ka.astype(ka_ref.dtype)


def _ext_placement():
    pq = np.zeros((3, HEADS, LANE), np.float32)
    pk = np.zeros((3, HEADS, LANE), np.float32)
    oq = np.zeros((1, LANE), np.float32)
    ok = np.zeros((1, LANE), np.float32)
    for h in range(HEADS):
        for j in range(3):
            pq[j, h, h * EXT_COLS + j] = 1.0
            pk[j, h, h * EXT_COLS + 3 + j] = 1.0
            oq[0, h * EXT_COLS + 3 + j] = 1.0
            ok[0, h * EXT_COLS + j] = 1.0
    return jnp.asarray(pq), jnp.asarray(pk), jnp.asarray(oq), jnp.asarray(ok)


def _cumsum_ext(logf, blk):
    b, l, h = logf.shape
    width = LANE
    pq, pk, oq, ok = _ext_placement()
    pspec = pl.BlockSpec((3, HEADS, width), lambda bi, i: (0, 0, 0))
    ospec = pl.BlockSpec((1, width), lambda bi, i: (0, 0))
    out = jax.ShapeDtypeStruct((b, l, width), BF16)
    return pl.pallas_call(
        functools.partial(_cumsum_kernel, blk=blk),
        grid=(b, l // blk),
        in_specs=[pl.BlockSpec((1, blk, h), lambda bi, i: (bi, i, 0)), pspec, pspec, ospec, ospec],
        out_specs=[pl.BlockSpec((1, blk, width), lambda bi, i: (bi, i, 0))] * 2,
        out_shape=[out, out],
        scratch_shapes=[pltpu.VMEM((1, h), F32)],
        compiler_params=_cparams(("parallel", "arbitrary")),
        name="cumsum",
    )(logf, pq, pk, oq, ok)


def _attn_kernel(*refs, fox, tq, tk, nk, past, chunk_log2, kv_len, l_pad):
    if fox:
        q_ref, qx_ref, k_ref, kx_ref, v_ref, o_ref, m_sc, acc_sc = refs
    else:
        q_ref, k_ref, kx_ref, v_ref, o_ref, m_sc, acc_sc = refs
    qi = pl.program_id(1)
    ki = pl.program_id(2)
    q_lo = past + qi * tq
    q_hi = q_lo + (tq - 1)
    k_lo = ki * tk
    k_hi = k_lo + (tk - 1)
    if nk == 1:
        last_k = 0
    else:
        last_k = jnp.minimum(nk - 1, ((((q_hi >> chunk_log2) + 1) << chunk_log2) - 1) // tk)
    needed = ki <= last_k
    full = jnp.logical_and((k_hi >> chunk_log2) <= (q_lo >> chunk_log2), k_hi < kv_len)

    @pl.when(ki == 0)
    def _init():
        m_sc[...] = jnp.full_like(m_sc, NEG_INF)
        acc_sc[...] = jnp.zeros_like(acc_sc)

    def step(masked):
        if masked:
            rows = q_lo + lax.broadcasted_iota(jnp.int32, (tq, tk), 0)
            cols = k_lo + lax.broadcasted_iota(jnp.int32, (tq, tk), 1)
            ok = (cols >> chunk_log2) <= (rows >> chunk_log2)
            if l_pad != kv_len:
                ok = jnp.logical_and(ok, cols < kv_len)
        ext_lane = lax.broadcasted_iota(jnp.int32, (tk, LANE), 1)
        ones_col = jnp.where(ext_lane == 0, 1.0, 0.0).astype(BF16)
        for h in range(HEADS):
            hs = slice(h * HEAD_DIM, (h + 1) * HEAD_DIM)
            if fox:
                q = jnp.concatenate([q_ref[0, :, hs], qx_ref[0]], axis=-1)
                kx = kx_ref[0]
                kx = jnp.where(ext_lane // EXT_COLS == h, kx, jnp.zeros_like(kx))
                k = jnp.concatenate([k_ref[0, :, hs], kx], axis=-1)
            else:
                q = q_ref[0, :, 2 * h * LANE:2 * (h + 1) * LANE]
                k = jnp.concatenate([k_ref[0, :, hs], kx_ref[0]], axis=-1)
            s = lax.dot_general(q, k, (((1,), (1,)), ((), ())), preferred_element_type=F32)
            if masked:
                s = jnp.where(ok, s, NEG_INF)
            m_prev = m_sc[h]
            m_new = jnp.maximum(m_prev, jnp.max(s, axis=-1, keepdims=True))
            alpha = jnp.exp2(m_prev - m_new)
            p = jnp.exp2(s - jnp.tile(m_new, (1, tk // LANE)))
            v = jnp.concatenate([v_ref[0, :, hs], ones_col], axis=-1)
            acc_sc[h] = jnp.tile(alpha, (1, 2)) * acc_sc[h] + jnp.dot(
                p.astype(BF16), v, preferred_element_type=F32)
            m_sc[h] = m_new

    @pl.when(jnp.logical_and(needed, full))
    def _full():
        step(False)

    @pl.when(jnp.logical_and(needed, jnp.logical_not(full)))
    def _masked():
        step(True)

    @pl.when(ki == last_k)
    def _fin():
        for h in range(HEADS):
            hs = slice(h * HEAD_DIM, (h + 1) * HEAD_DIM)
            acc = acc_sc[h]
            o_ref[0, :, hs] = (acc[:, :HEAD_DIM] / acc[:, HEAD_DIM:HEAD_DIM + 1]).astype(o_ref.dtype)


def _attention(q, k, v, extra, *, fox, tq, tk, past, kv_len, v_col=0):
    b, sq, qcols = q.shape
    l_pad = k.shape[1]
    nq, nk = sq // tq, l_pad // tk
    chunk_log2 = 0 if fox else CHUNK_LOG2
    width = HEADS * HEAD_DIM

    def kmap(bi, qi, ki):
        if nk == 1:
            return 0
        q_hi = past + qi * tq + (tq - 1)
        last_k = jnp.minimum(nk - 1, ((((q_hi >> chunk_log2) + 1) << chunk_log2) - 1) // tk)
        return jnp.minimum(ki, last_k)

    q_spec = pl.BlockSpec((1, tq, qcols), lambda bi, qi, ki: (bi, qi, 0))
    k_spec = pl.BlockSpec((1, tk, width), lambda bi, qi, ki: (bi, kmap(bi, qi, ki), 0))
    v_spec = pl.BlockSpec((1, tk, width), lambda bi, qi, ki: (bi, kmap(bi, qi, ki), v_col))
    kx_spec = pl.BlockSpec((1, tk, LANE), lambda bi, qi, ki: (bi, kmap(bi, qi, ki), 0))
    if fox:
        q_blk_off = past // tq
        qx_spec = pl.BlockSpec((1, tq, LANE), lambda bi, qi, ki: (bi, q_blk_off + qi, 0))
        in_specs = [q_spec, qx_spec, k_spec, kx_spec, v_spec]
        operands = (q, extra[0], k, extra[1], v)
    else:
        in_specs = [q_spec, k_spec, kx_spec, v_spec]
        operands = (q, k, extra[0], v)
    return pl.pallas_call(
        functools.partial(_attn_kernel, fox=fox, tq=tq, tk=tk, nk=nk, past=past,
                          chunk_log2=chunk_log2, kv_len=kv_len, l_pad=l_pad),
        grid=(b, nq, nk),
        in_specs=in_specs,
        out_specs=pl.BlockSpec((1, tq, width), lambda bi, qi, ki: (bi, qi, 0)),
        out_shape=jax.ShapeDtypeStruct((b, sq, width), BF16),
        scratch_shapes=[pltpu.VMEM((HEADS, tq, LANE), F32),
                        pltpu.VMEM((HEADS, tq, 2 * LANE), F32)],
        compiler_params=_cparams(("parallel", "parallel", "arbitrary")),
        name="attn_fox" if fox else "attn_mla",
    )(*operands)


def _merge_kernel(om_ref, of_ref, wm_ref, wf_ref, gm_ref, gf_ref, o_ref):
    a = jnp.dot(om_ref[...], wm_ref[...], preferred_element_type=F32)
    b = jnp.dot(of_ref[...], wf_ref[...], preferred_element_type=F32)
    o_ref[...] = (gm_ref[...].astype(F32) * a + gf_ref[...].astype(F32) * b).astype(o_ref.dtype)


def _merge(o_mla, o_fox, w_o_mla, w_o_fox, gates, tm, tn):
    t, width = o_mla.shape
    ncol = D_MODEL // tn
    return pl.pallas_call(
        _merge_kernel,
        grid=(t // tm, ncol),
        in_specs=[pl.BlockSpec((tm, width), lambda i, j: (i, 0)),
                  pl.BlockSpec((tm, width), lambda i, j: (i, 0)),
                  pl.BlockSpec((width, tn), lambda i, j: (0, j)),
                  pl.BlockSpec((width, tn), lambda i, j: (0, j)),
                  pl.BlockSpec((tm, tn), lambda i, j: (i, j)),
                  pl.BlockSpec((tm, tn), lambda i, j: (i, ncol + j))],
        out_specs=pl.BlockSpec((tm, tn), lambda i, j: (i, j)),
        out_shape=jax.ShapeDtypeStruct((t, D_MODEL), BF16),
        compiler_params=_cparams(("parallel", "parallel")),
        name="merge",
    )(o_mla, o_fox, w_o_mla, w_o_fox, gates, gates)


def _outproj_kernel(x_ref, mg_ref, w_ref, g1_ref, n2_ref, sh_ref, sc_ref, x1_ref, h2_ref, h2t_ref):
    y = jnp.dot(mg_ref[...], w_ref[...], preferred_element_type=F32)
    x1 = x_ref[...] + _mod_rows(g1_ref) * y
    x1_ref[...] = x1
    n = x1 * lax.rsqrt(jnp.mean(x1 * x1, axis=-1, keepdims=True) + EPS) * n2_ref[...]
    h2 = n * (1.0 + _mod_rows(sc_ref)) + _mod_rows(sh_ref)
    h2_ref[...] = h2.astype(h2_ref.dtype)
    h2t_ref[...] = h2.T.astype(h2t_ref.dtype)


def _outproj(x, merged, w_out, gate1, norm2_g, shift2, scale2, seq, tm):
    t = x.shape[0]
    row = pl.BlockSpec((tm, D_MODEL), lambda i: (i, 0))
    return pl.pallas_call(
        _outproj_kernel,
        grid=(t // tm,),
        in_specs=[row, row,
                  pl.BlockSpec((D_MODEL, D_MODEL), lambda i: (0, 0)),
                  _mod_spec(gate1, tm, seq),
                  pl.BlockSpec((1, D_MODEL), lambda i: (0, 0)),
                  _mod_spec(shift2, tm, seq), _mod_spec(scale2, tm, seq)],
        out_specs=[row, row, pl.BlockSpec((D_MODEL, tm), lambda i: (0, i))],
        out_shape=[jax.ShapeDtypeStruct((t, D_MODEL), F32),
                   jax.ShapeDtypeStruct((t, D_MODEL), BF16),
                   jax.ShapeDtypeStruct((D_MODEL, t), BF16)],
        compiler_params=_cparams(("parallel",)),
        name="outproj",
    )(x, merged, w_out, gate1, norm2_g, shift2, scale2)


def _peer_score_kernel(h_ref, wq_ref, k1_ref, k2_ref, s1_ref, s2_ref):
    q = jnp.dot(h_ref[...], wq_ref[...], preferred_element_type=F32).astype(BF16)
    nt = (((1,), (1,)), ((), ()))
    half = N_KEYS
    for h in range(HEADS):
        base = h * 2 * half
        s1_ref[h] = lax.dot_general(k1_ref[...], q[:, base:base + half], nt,
                                    preferred_element_type=F32)
        s2_ref[h] = lax.dot_general(k2_ref[...], q[:, base + half:base + 2 * half], nt,
                                    preferred_element_type=F32)


def _peer_scores(h2, wq, k1, k2, tm):
    t = h2.shape[0]
    out = jax.ShapeDtypeStruct((HEADS, N_KEYS, t), F32)
    ospec = pl.BlockSpec((HEADS, N_KEYS, tm), lambda i: (0, 0, i))
    kspec = pl.BlockSpec((N_KEYS, N_KEYS), lambda i: (0, 0))
    return pl.pallas_call(
        _peer_score_kernel,
        grid=(t // tm,),
        in_specs=[pl.BlockSpec((tm, D_MODEL), lambda i: (i, 0)),
                  pl.BlockSpec((D_MODEL, D_MODEL), lambda i: (0, 0)),
                  kspec, kspec],
        out_specs=[ospec, ospec],
        out_shape=[out, out],
        compiler_params=_cparams(("parallel",)),
        name="peer_scores",
    )(h2, wq, k1, k2)


_N_RANK = PEER_TOPK + 1
_CAND_PAIRS = [(i, j) for i in range(_N_RANK) for j in range(_N_RANK) if (i + 1) * (j + 1) <= _N_RANK]
_CAND_ROWS = -(-len(_CAND_PAIRS) // 8) * 8


_NO_RANK = 127.0


def _top_values(x, n, ranked=False):
    vals = []
    rank = jnp.full(x.shape, _NO_RANK, F32) if ranked else None
    for r in range(n):
        mx = jnp.max(x, axis=0, keepdims=True)
        vals.append(mx)
        hit = x == mx
        if ranked:
            rank = jnp.where(hit, float(r), rank)
        x = jnp.where(hit, -jnp.inf, x)
    return (vals, rank) if ranked else vals


def _peer_topk_kernel(s1_ref, s2_ref, c1_ref, n1_ref, r2_ref, e2_ref, cand_sc):
    for h in range(HEADS):
        a = s1_ref[h]
        b = s2_ref[h]
        v1 = _top_values(a, _N_RANK)
        v2, rank2 = _top_values(b, _N_RANK, ranked=True)
        cand_sc[...] = jnp.full_like(cand_sc, -jnp.inf)
        for r, (i, j) in enumerate(_CAND_PAIRS):
            cand_sc[r:r + 1, :] = v1[i] + v2[j]
        tops = _top_values(cand_sc[...], _N_RANK)
        tau = 0.5 * (tops[PEER_TOPK - 1] + tops[PEER_TOPK])
        z = jnp.exp(tops[0] - tops[0])
        for t in tops[1:PEER_TOPK]:
            z = z + jnp.exp(t - tops[0])
        n1 = jnp.zeros_like(a)
        for j in range(PEER_TOPK):
            n1 = n1 + jnp.where(a >= tau - v2[j], 1.0, 0.0)
        c1_ref[h] = jnp.exp(a - v1[0]) / z
        n1_ref[h] = n1
        r2_ref[h] = rank2.astype(r2_ref.dtype)
        e2_ref[h] = jnp.exp(b - v2[0]).astype(e2_ref.dtype)


def _peer_topk(s1t, s2t, tl):
    t = s1t.shape[2]
    spec = pl.BlockSpec((HEADS, N_KEYS, tl), lambda i: (0, 0, i))
    f32 = jax.ShapeDtypeStruct(s1t.shape, F32)
    bf16 = jax.ShapeDtypeStruct(s1t.shape, BF16)
    return pl.pallas_call(
        _peer_topk_kernel,
        grid=(t // tl,),
        in_specs=[spec, spec],
        out_specs=[spec, spec, spec, spec],
        out_shape=[f32, f32, bf16, bf16],
        scratch_shapes=[pltpu.VMEM((_CAND_ROWS, tl), F32)],
        compiler_params=_cparams(("parallel",)),
        name="peer_topk",
    )(s1t, s2t)


def _peer_dense_kernel(ht_ref, u_ref, vt_ref, c1_ref, n1_ref, r2_ref, e2_ref, o_ref,
                       a_sc, w_sc, acc_sc):
    c = pl.program_id(1)

    @pl.when(c == 0)
    def _():
        acc_sc[...] = jnp.zeros_like(acc_sc)

    a_sc[...] = jnp.dot(u_ref[...], ht_ref[...], preferred_element_type=F32)
    zero = jnp.zeros((), BF16)
    tb = ht_ref.shape[1]

    def row_bf16(ref, h, ii):
        tile = jnp.broadcast_to(ref[h, ii:ii + 1, :], (16, tb)).astype(BF16)
        return jnp.tile(tile, (N_KEYS // 16, 1))

    for ii in range(EXPERT_CHUNK // N_KEYS):
        rows = slice(ii * N_KEYS, (ii + 1) * N_KEYS)
        g = None
        for h in range(HEADS):
            n = row_bf16(n1_ref, h, ii)
            cf = row_bf16(c1_ref, h, ii)
            term = jnp.where(r2_ref[h] < n, e2_ref[h] * cf, zero)
            g = term if g is None else g + term
        a = a_sc[rows, :]
        gelu = a * (lax.erf(a / math.sqrt(2.0)) + 1.0) / 2.0
        w_sc[rows, :] = g * gelu.astype(BF16)
    acc_sc[...] += jnp.dot(vt_ref[...], w_sc[...], preferred_element_type=F32)

    @pl.when(c == pl.num_programs(1) - 1)
    def _():
        o_ref[...] = acc_sc[...].T


def _peer_dense(h2t, u_bf, vt_bf, c1t, n1t, r2t, e2t, tb):
    t = h2t.shape[1]
    n_exp = u_bf.shape[0]
    rows = EXPERT_CHUNK // N_KEYS
    small = pl.BlockSpec((HEADS, rows, tb), lambda i, c: (0, c, i))
    big = pl.BlockSpec((HEADS, N_KEYS, tb), lambda i, c: (0, 0, i))
    return pl.pallas_call(
        _peer_dense_kernel,
        grid=(t // tb, n_exp // EXPERT_CHUNK),
        in_specs=[pl.BlockSpec((D_MODEL, tb), lambda i, c: (0, i)),
                  pl.BlockSpec((EXPERT_CHUNK, D_MODEL), lambda i, c: (c, 0)),
                  pl.BlockSpec((D_MODEL, EXPERT_CHUNK), lambda i, c: (0, c)),
                  small, small, big, big],
        out_specs=pl.BlockSpec((tb, D_MODEL), lambda i, c: (i, 0)),
        out_shape=jax.ShapeDtypeStruct((t, D_MODEL), F32),
        scratch_shapes=[pltpu.VMEM((EXPERT_CHUNK, tb), F32),
                        pltpu.VMEM((EXPERT_CHUNK, tb), BF16),
                        pltpu.VMEM((D_MODEL, tb), F32)],
        compiler_params=_cparams(("parallel", "arbitrary")),
        name="peer_dense",
    )(h2t, u_bf, vt_bf, c1t, n1t, r2t, e2t)


def _final_kernel(x_ref, p_ref, g2_ref, fg_ref, o_ref):
    x = x_ref[...] + _mod_rows(g2_ref) * p_ref[...]
    o_ref[...] = x * lax.rsqrt(jnp.mean(x * x, axis=-1, keepdims=True) + EPS) * fg_ref[...]


def _final(x1, peer, gate2, final_g, seq, tm):
    t = x1.shape[0]
    row = pl.BlockSpec((tm, D_MODEL), lambda i: (i, 0))
    return pl.pallas_call(
        _final_kernel,
        grid=(t // tm,),
        in_specs=[row, row, _mod_spec(gate2, tm, seq), pl.BlockSpec((1, D_MODEL), lambda i: (0, 0))],
        out_specs=row,
        out_shape=jax.ShapeDtypeStruct((t, D_MODEL), F32),
        compiler_params=_cparams(("parallel",)),
        name="final",
    )(x1, peer, gate2, final_g)


def _rope_tables(pos):
    inv = 1.0 / (ROPE_THETA ** (jnp.arange(0, QK_ROPE, 2, dtype=F32) / QK_ROPE))
    ang = pos.astype(F32)[:, None] * inv[None, :]
    cos, sin = jnp.cos(ang), jnp.sin(ang)
    z = jnp.zeros_like(cos)
    return (jnp.concatenate([cos, cos, z, z], axis=1),
            jnp.concatenate([-sin, z, z, z], axis=1),
            jnp.concatenate([z, sin, z, z], axis=1))


def _prep_weights(w_in, fox_fbias, w_ukv, w_o_mla, w_o_fox, w_out, peer_wq, peer_keys1, peer_keys2,
                  peer_u, peer_v):
    d = D_MODEL
    n_q = HEADS * (HEAD_DIM + QK_ROPE)
    wq = w_in[:, :n_q].reshape(d, HEADS, HEAD_DIM + QK_ROPE)
    wq_cat = jnp.concatenate([wq, jnp.zeros((d, HEADS, QK_ROPE), F32)], axis=-1).reshape(d, HEADS * 2 * LANE)
    o1 = n_q + KV_LORA
    o2 = o1 + QK_ROPE
    o3 = o2 + 3 * HEADS * HEAD_DIM
    o4 = o3 + HEADS
    w_kv = jnp.concatenate([w_in[:, n_q:o2], jnp.zeros((d, LANE - QK_ROPE), F32),
                            w_in[:, o3:o4], jnp.zeros((d, LANE - HEADS), F32)], axis=1)
    fb = jnp.concatenate([fox_fbias, jnp.zeros((LANE - HEADS,), F32)])[None, :]
    w_ukv3 = w_ukv.reshape(KV_LORA, HEADS, 2 * HEAD_DIM)
    w_ukv_r = jnp.concatenate([w_ukv3[:, :, :HEAD_DIM].reshape(KV_LORA, -1),
                               w_ukv3[:, :, HEAD_DIM:].reshape(KV_LORA, -1)], axis=1)
    return dict(
        wq_cat=wq_cat.astype(BF16), w_kv=w_kv.astype(BF16), fb=fb,
        w_fox=w_in[:, o2:o3].astype(BF16), w_gate=w_in[:, o4:].astype(BF16),
        w_ukv=w_ukv_r.astype(BF16), w_o_mla=w_o_mla.astype(BF16), w_o_fox=w_o_fox.astype(BF16),
        w_out=w_out.astype(BF16), peer_wq=peer_wq.astype(BF16),
        k1=peer_keys1.astype(BF16), k2=peer_keys2.astype(BF16),
        u=peer_u.astype(BF16), vt=peer_v.T.astype(BF16))


def _layer(x, mods, caches, wts, norm1_g, kv_norm_g, norm2_g, final_g, *, batch, seq, past, tm, tq, tk):
    t = batch * seq
    shift1, scale1, gate1, shift2, scale2, gate2 = mods
    width = HEADS * HEAD_DIM
    pos = past + jnp.arange(seq, dtype=jnp.int32)
    tabs = _rope_tables(pos)
    if tm > seq:
        tabs = tuple(jnp.tile(tb, (tm // seq, 1)) for tb in tabs)

    h = _norm_mod(x, norm1_g, shift1, scale1, seq, min(tm, 512))
    q_mla = _proj_qmla(h, wts["wq_cat"], tabs, tm, LOG2E * (HEAD_DIM + QK_ROPE) ** -0.5)
    lat, lat_b, k_r, k_rp, logf = _proj_kv(h, wts["w_kv"], kv_norm_g, wts["fb"], tabs, tm)
    w_fox = wts["w_fox"]
    (q_fox,) = _proj(h, w_fox[:, :width], [BF16], tm, width, scale=LOG2E * HEAD_DIM ** -0.5,
                      name="proj_foxq")
    k_new, k_new_b = _proj(h, w_fox[:, width:2 * width], [F32, BF16], tm, width, name="proj_foxk")
    v_new, v_new_b = _proj(h, w_fox[:, 2 * width:], [F32, BF16], tm, width, name="proj_foxv")
    (gates,) = _proj(h, wts["w_gate"], [BF16], tm, 1024, act="sigmoid", name="proj_gate")

    kv_len = past + seq
    l_pad = -(-kv_len // tk) * tk

    def with_cache(cache, new, cols):
        new = new.reshape(batch, seq, cols)
        if cache is None:
            return new
        parts = [cache.reshape(batch, past, cols).astype(new.dtype), new]
        if l_pad > kv_len:
            parts.append(jnp.zeros((batch, l_pad - kv_len, cols), new.dtype))
        return jnp.concatenate(parts, axis=1)

    c_lat, c_kr, c_k, c_v, c_logf = caches if caches is not None else (None,) * 5
    lat_all = with_cache(c_lat, lat_b, KV_LORA)
    kr_all = with_cache(None if c_kr is None else jnp.pad(c_kr, ((0, 0), (0, 0), (0, LANE - QK_ROPE))),
                        k_rp, LANE)
    k_all = with_cache(c_k, k_new_b, width)
    v_all = with_cache(c_v, v_new_b, width)
    logf_all = with_cache(c_logf, logf, HEADS)

    (kv_up,) = _proj(lat_all.reshape(batch * l_pad, KV_LORA), wts["w_ukv"], [BF16],
                     min(1024, batch * l_pad), 2 * width, name="kv_up")
    kv_up = kv_up.reshape(batch, l_pad, 2 * width)
    o_mla = _attention(q_mla.reshape(batch, seq, -1), kv_up, kv_up, (kr_all,), fox=False,
                       tq=tq, tk=tk, past=past, kv_len=kv_len, v_col=1)

    q_ext, k_ext = _cumsum_ext(logf_all, tk)
    o_fox = _attention(q_fox.reshape(batch, seq, width), k_all, v_all, (q_ext, k_ext), fox=True,
                       tq=tq, tk=tk, past=past, kv_len=kv_len)

    merged = _merge(o_mla.reshape(t, width), o_fox.reshape(t, width), wts["w_o_mla"], wts["w_o_fox"],
                    gates, tm, 1024)
    tm2 = 512 if gate1.ndim == 3 else 256
    x1, h2, h2t = _outproj(x, merged, wts["w_out"], gate1, norm2_g, shift2, scale2, seq, tm2)

    s1t, s2t = _peer_scores(h2, wts["peer_wq"], wts["k1"], wts["k2"], tm2)
    c1t, n1t, r2t, e2t = _peer_topk(s1t, s2t, 256)
    peer = _peer_dense(h2t, wts["u"], wts["vt"], c1t, n1t, r2t, e2t, 512)
    y = _final(x1, peer, gate2, final_g, seq, tm2)
    return y, (lat, k_r, k_new, v_new, logf)


def kernel(x_prompt, x_sample, c_prompt, c_sample, cache_mla_latent, cache_mla_krope, cache_fox_k,
           cache_fox_v, cache_fox_logf, w_ada, b_ada, norm1_g, w_in, fox_fbias, kv_norm_g, w_ukv,
           w_o_mla, w_o_fox, w_out, norm2_g, peer_wq, peer_keys1, peer_keys2, peer_u, peer_v, final_g):
    depth = w_ada.shape[0]
    assert depth == 1, "the final RMSNorm is fused into the single trunk layer"
    bp, sp, d = x_prompt.shape
    bs, ss, _ = x_sample.shape
    past = cache_mla_latent.shape[2]
    xp = x_prompt.reshape(bp * sp, d)
    xs = x_sample.reshape(bs * ss, d)
    c_rows = -(-(bp + bs) // 16) * 16
    c_all = jnp.concatenate([c_prompt, c_sample, jnp.zeros((c_rows - bp - bs, d), F32)], axis=0)
    fg = final_g[None, :]
    states_p, states_s = [], []
    for l in range(depth):
        wts = _prep_weights(w_in[l], fox_fbias[l], w_ukv[l], w_o_mla[l], w_o_fox[l], w_out[l],
                            peer_wq[l], peer_keys1[l], peer_keys2[l], peer_u[l], peer_v[l])
        mod = _ada(c_all, w_ada[l], b_ada[l][None, :])
        mods_p = tuple(m[:, None, :] for m in jnp.split(mod[:bp], N_MOD, axis=-1))
        mods_s = tuple(jnp.repeat(m, ss, axis=0) for m in jnp.split(mod[bp:bp + bs], N_MOD, axis=-1))
        n1, kvg, n2 = norm1_g[l][None, :], kv_norm_g[l][None, :], norm2_g[l][None, :]
        xp, st_p = _layer(xp, mods_p, None, wts, n1, kvg, n2, fg,
                          batch=bp, seq=sp, past=0, tm=1024, tq=512, tk=512)
        caches = (cache_mla_latent[l], cache_mla_krope[l], cache_fox_k[l], cache_fox_v[l], cache_fox_logf[l])
        xs, st_s = _layer(xs, mods_s, caches, wts, n1, kvg, n2, fg,
                          batch=bs, seq=ss, past=past, tm=1024, tq=ss, tk=1152)
        states_p.append(st_p)
        states_s.append(st_s)

    def stack(states, i, batch, seq, tail):
        return jnp.stack([st[i].reshape((batch, seq) + tail) for st in states])

    tails = [(KV_LORA,), (QK_ROPE,), (HEADS, HEAD_DIM), (HEADS, HEAD_DIM), (HEADS,)]
    outs_p = [stack(states_p, i, bp, sp, tails[i]) for i in range(5)]
    outs_s = [stack(states_s, i, bs, ss, tails[i]) for i in range(5)]
    return (xp.reshape(bp, sp, d), xs.reshape(bs, ss, d), *outs_p, *outs_s)
```

```python
import functools
import math

import jax
import jax.numpy as jnp
import numpy as np
from jax import lax
from jax.experimental import pallas as pl
from jax.experimental.pallas import tpu as pltpu

F32 = jnp.float32
BF16 = jnp.bfloat16

D_MODEL = 2048
EPS = 1e-6
NEG_INF = -1e30
LOG2E = 1.4426950408889634
N_MOD = 6
HEADS = 8
HEAD_DIM = 128
QK_ROPE = 64
KV_LORA = 512
CHUNK_LOG2 = 6
ROPE_THETA = 10000.0
N_KEYS = 128
PEER_TOPK = 16
LANE = 128
EXT_COLS = 8
EXPERT_CHUNK = 1024
ACT_SPLIT = 4
OUT_SPLIT = 2
VMEM_LIMIT_MB = 56


def _cparams(sem, vmem_mb=VMEM_LIMIT_MB, flags=None):
    return pltpu.CompilerParams(dimension_semantics=sem, vmem_limit_bytes=vmem_mb * 1024 * 1024,
                                flags=flags)


def _mod_rows(ref):
    v = ref[...]
    return v[0] if v.ndim == 3 else v


def _mod_spec(mod, tm, seq):
    if mod.ndim == 3:
        blocks_per_batch = seq // tm
        return pl.BlockSpec((1, 1, D_MODEL), lambda i, *_: (i // blocks_per_batch, 0, 0))
    return pl.BlockSpec((tm, D_MODEL), lambda i, *_: (i, 0))


def _rope128(r, c, t1, t2):
    return r * c + pltpu.roll(r, 96, 1) * t1 + pltpu.roll(r, 32, 1) * t2


def _ada_kernel(c_ref, w_ref, b_ref, o_ref):
    c = c_ref[...]
    a = (c * jax.nn.sigmoid(c)).astype(BF16)
    o_ref[...] = jnp.dot(a, w_ref[...].astype(BF16), preferred_element_type=F32) + b_ref[...]


def _ada(c_all, w_ada, b_ada):
    rows, n = c_all.shape[0], w_ada.shape[1]
    tn = 1024
    return pl.pallas_call(
        _ada_kernel,
        grid=(n // tn,),
        in_specs=[pl.BlockSpec((rows, D_MODEL), lambda j: (0, 0)),
                  pl.BlockSpec((D_MODEL, tn), lambda j: (0, j)),
                  pl.BlockSpec((1, tn), lambda j: (0, j))],
        out_specs=pl.BlockSpec((rows, tn), lambda j: (0, j)),
        out_shape=jax.ShapeDtypeStruct((rows, n), F32),
        compiler_params=_cparams(("parallel",)),
        name="ada",
    )(c_all, w_ada, b_ada)


def _norm_mod_kernel(x_ref, g_ref, sh_ref, sc_ref, o_ref):
    x = x_ref[...]
    y = x * lax.rsqrt(jnp.mean(x * x, axis=-1, keepdims=True) + EPS) * g_ref[...]
    o_ref[...] = (y * (1.0 + _mod_rows(sc_ref)) + _mod_rows(sh_ref)).astype(o_ref.dtype)


def _norm_mod(x, g, shift, scale, seq, tm):
    t = x.shape[0]
    return pl.pallas_call(
        _norm_mod_kernel,
        grid=(t // tm,),
        in_specs=[pl.BlockSpec((tm, D_MODEL), lambda i: (i, 0)),
                  pl.BlockSpec((1, D_MODEL), lambda i: (0, 0)),
                  _mod_spec(shift, tm, seq), _mod_spec(scale, tm, seq)],
        out_specs=pl.BlockSpec((tm, D_MODEL), lambda i: (i, 0)),
        out_shape=jax.ShapeDtypeStruct((t, D_MODEL), BF16),
        compiler_params=_cparams(("parallel",)),
        name="norm_mod",
    )(x, g, shift, scale)


def _proj_kernel(h_ref, w_ref, *o_refs, scale, act):
    acc = jnp.dot(h_ref[...], w_ref[...], preferred_element_type=F32)
    if scale != 1.0:
        acc = acc * scale
    if act == "sigmoid":
        acc = jax.nn.sigmoid(acc)
    for o_ref in o_refs:
        o_ref[...] = acc.astype(o_ref.dtype)


def _proj(h, w, out_dtypes, tm, tn, scale=1.0, act=None, name="proj"):
    t, k = h.shape
    n = w.shape[1]
    outs = pl.pallas_call(
        functools.partial(_proj_kernel, scale=scale, act=act),
        grid=(t // tm, n // tn),
        in_specs=[pl.BlockSpec((tm, k), lambda i, j: (i, 0)),
                  pl.BlockSpec((k, tn), lambda i, j: (0, j))],
        out_specs=[pl.BlockSpec((tm, tn), lambda i, j: (i, j)) for _ in out_dtypes],
        out_shape=[jax.ShapeDtypeStruct((t, n), dt) for dt in out_dtypes],
        compiler_params=_cparams(("parallel", "parallel")),
        name=name,
    )(h, w)
    return outs


def _proj_qmla_kernel(h_ref, w_ref, c_ref, t1_ref, t2_ref, o_ref, *, scale):
    acc = jnp.dot(h_ref[...], w_ref[...], preferred_element_type=F32) * scale
    for hh in range(acc.shape[1] // (2 * LANE)):
        base = hh * 2 * LANE
        rot = _rope128(acc[:, base + LANE:base + 2 * LANE], c_ref[...], t1_ref[...], t2_ref[...])
        o_ref[:, base:base + LANE] = acc[:, base:base + LANE].astype(o_ref.dtype)
        o_ref[:, base + LANE:base + 2 * LANE] = rot.astype(o_ref.dtype)


def _proj_qmla(h, w, tabs, tm, scale, heads_per_step=4):
    t = h.shape[0]
    ntab = tabs[0].shape[0] // tm
    tn = heads_per_step * 2 * LANE
    tab_spec = pl.BlockSpec((tm, LANE), lambda i, j: (i % ntab, 0))
    return pl.pallas_call(
        functools.partial(_proj_qmla_kernel, scale=scale),
        grid=(t // tm, HEADS // heads_per_step),
        in_specs=[pl.BlockSpec((tm, D_MODEL), lambda i, j: (i, 0)),
                  pl.BlockSpec((D_MODEL, tn), lambda i, j: (0, j)),
                  tab_spec, tab_spec, tab_spec],
        out_specs=pl.BlockSpec((tm, tn), lambda i, j: (i, j)),
        out_shape=jax.ShapeDtypeStruct((t, HEADS * 2 * LANE), BF16),
        compiler_params=_cparams(("parallel", "parallel")),
        name="proj_qmla",
    )(h, w, *tabs)


def _proj_kv_kernel(h_ref, w_ref, g_ref, fb_ref, c_ref, t1_ref, t2_ref,
                    lat_ref, latb_ref, kr_ref, krp_ref, logf_ref):
    acc = jnp.dot(h_ref[...], w_ref[...], preferred_element_type=F32)
    ckv = acc[:, :KV_LORA]
    lat = ckv * lax.rsqrt(jnp.mean(ckv * ckv, axis=-1, keepdims=True) + EPS) * g_ref[...]
    lat_ref[...] = lat
    latb_ref[...] = lat.astype(latb_ref.dtype)
    rot = _rope128(acc[:, KV_LORA:KV_LORA + LANE], c_ref[...], t1_ref[...], t2_ref[...])
    kr_ref[...] = rot[:, :QK_ROPE]
    krp_ref[...] = rot.astype(krp_ref.dtype)
    f = acc[:, KV_LORA + LANE:] + fb_ref[...]
    logf = jnp.minimum(f, 0.0) - jnp.log1p(jnp.exp(-jnp.abs(f)))
    logf_ref[...] = logf[:, :HEADS]


def _proj_kv(h, w, g, fb, tabs, tm):
    t = h.shape[0]
    n = w.shape[1]
    ntab = tabs[0].shape[0] // tm
    tab_spec = pl.BlockSpec((tm, LANE), lambda i: (i % ntab, 0))
    row = lambda width: pl.BlockSpec((tm, width), lambda i: (i, 0))
    return pl.pallas_call(
        _proj_kv_kernel,
        grid=(t // tm,),
        in_specs=[row(D_MODEL),
                  pl.BlockSpec((D_MODEL, n), lambda i: (0, 0)),
                  pl.BlockSpec((1, KV_LORA), lambda i: (0, 0)),
                  pl.BlockSpec((1, LANE), lambda i: (0, 0)),
                  tab_spec, tab_spec, tab_spec],
        out_specs=[row(KV_LORA), row(KV_LORA), row(QK_ROPE), row(LANE), row(HEADS)],
        out_shape=[jax.ShapeDtypeStruct((t, KV_LORA), F32),
                   jax.ShapeDtypeStruct((t, KV_LORA), BF16),
                   jax.ShapeDtypeStruct((t, QK_ROPE), F32),
                   jax.ShapeDtypeStruct((t, LANE), BF16),
                   jax.ShapeDtypeStruct((t, HEADS), F32)],
        compiler_params=_cparams(("parallel",)),
        name="proj_kv",
    )(h, w, g, fb, *tabs)


def _split3(x):
    hi = x.astype(BF16).astype(F32)
    r1 = x - hi
    mid = r1.astype(BF16).astype(F32)
    lo = (r1 - mid).astype(BF16).astype(F32)
    return hi, mid, lo


def _cumsum_kernel(x_ref, pq_ref, pk_ref, oq_ref, ok_ref, qa_ref, ka_ref, carry_ref, *, blk):
    @pl.when(pl.program_id(1) == 0)
    def _():
        carry_ref[...] = jnp.zeros_like(carry_ref)

    hi, mid, lo = _split3(x_ref[0])
    rows = lax.broadcasted_iota(jnp.int32, (blk, blk), 0)
    cols = lax.broadcasted_iota(jnp.int32, (blk, blk), 1)
    tril = jnp.where(cols <= rows, 1.0, 0.0).astype(BF16)
    y = (jnp.dot(tril, lo.astype(BF16), preferred_element_type=F32)
         + jnp.dot(tril, mid.astype(BF16), preferred_element_type=F32)
         + jnp.dot(tril, hi.astype(BF16), preferred_element_type=F32)) + carry_ref[...]
    carry_ref[...] = y[blk - 1:blk, :]
    pieces = _split3(y * LOG2E)
    qa = oq_ref[...]
    ka = ok_ref[...]
    for j, piece in enumerate(pieces):
        qa = qa + jnp.dot(piece, pq_ref[j], preferred_element_type=F32)
        ka = ka - jnp.dot(piece, pk_ref[j], preferred_element_type=F32)
    qa_ref[0] = qa.astype(qa_ref.dtype)
    ka_ref[0] = ka.astype(ka_ref.dtype)


def _ext_placement():
    pq = np.zeros((3, HEADS, LANE), np.float32)
    pk = np.zeros((3, HEADS, LANE), np.float32)
    oq = np.zeros((1, LANE), np.float32)
    ok = np.zeros((1, LANE), np.float32)
    for h in range(HEADS):
        for j in range(3):
            pq[j, h, h * EXT_COLS + j] = 1.0
            pk[j, h, h * EXT_COLS + 3 + j] = 1.0
            oq[0, h * EXT_COLS + 3 + j] = 1.0
            ok[0, h * EXT_COLS + j] = 1.0
    return jnp.asarray(pq), jnp.asarray(pk), jnp.asarray(oq), jnp.asarray(ok)


def _cumsum_ext(logf, blk):
    b, l, h = logf.shape
    width = LANE
    pq, pk, oq, ok = _ext_placement()
    pspec = pl.BlockSpec((3, HEADS, width), lambda bi, i: (0, 0, 0))
    ospec = pl.BlockSpec((1, width), lambda bi, i: (0, 0))
    out = jax.ShapeDtypeStruct((b, l, width), BF16)
    return pl.pallas_call(
        functools.partial(_cumsum_kernel, blk=blk),
        grid=(b, l // blk),
        in_specs=[pl.BlockSpec((1, blk, h), lambda bi, i: (bi, i, 0)), pspec, pspec, ospec, ospec],
        out_specs=[pl.BlockSpec((1, blk, width), lambda bi, i: (bi, i, 0))] * 2,
        out_shape=[out, out],
        scratch_shapes=[pltpu.VMEM((1, h), F32)],
        compiler_params=_cparams(("parallel", "arbitrary")),
        name="cumsum",
    )(logf, pq, pk, oq, ok)


def _pack_cache_kernel(ck_ref, cv_ref, nk_ref, nv_ref, ok_ref, ov_ref, *, past, seq):
    for c_ref, n_ref, o_ref in ((ck_ref, nk_ref, ok_ref), (cv_ref, nv_ref, ov_ref)):
        by_head = pltpu.einshape("thd->htd", c_ref[0])
        for h in range(HEADS):
            o_ref[0, :past, h * HEAD_DIM:(h + 1) * HEAD_DIM] = by_head[h].astype(o_ref.dtype)
        o_ref[0, past:past + seq, :] = n_ref[0]
        o_ref[0, past + seq:, :] = jnp.zeros((o_ref.shape[1] - past - seq, o_ref.shape[2]), o_ref.dtype)


def _pack_cache(cache_k, cache_v, new_k, new_v, l_pad):
    b, past = cache_k.shape[:2]
    seq, width = new_k.shape[1:]
    cspec = pl.BlockSpec((1, past, HEADS, HEAD_DIM), lambda bi: (bi, 0, 0, 0))
    nspec = pl.BlockSpec((1, seq, width), lambda bi: (bi, 0, 0))
    ospec = pl.BlockSpec((1, l_pad, width), lambda bi: (bi, 0, 0))
    out = jax.ShapeDtypeStruct((b, l_pad, width), BF16)
    return pl.pallas_call(
        functools.partial(_pack_cache_kernel, past=past, seq=seq),
        grid=(b,),
        in_specs=[cspec, cspec, nspec, nspec],
        out_specs=[ospec, ospec],
        out_shape=[out, out],
        compiler_params=_cparams(("parallel",)),
        name="pack_cache",
    )(cache_k, cache_v, new_k, new_v)


def _attn_kernel(*refs, fox, tq, tk, nk, past, chunk_log2, kv_len, l_pad):
    if fox:
        q_ref, qx_ref, k_ref, kx_ref, v_ref, o_ref, m_sc, acc_sc = refs
    else:
        q_ref, k_ref, kx_ref, v_ref, o_ref, m_sc, acc_sc = refs
    qi = pl.program_id(1)
    ki = pl.program_id(2)
    q_lo = past + qi * tq
    q_hi = q_lo + (tq - 1)
    k_lo = ki * tk
    k_hi = k_lo + (tk - 1)
    if nk == 1:
        last_k = 0
    else:
        last_k = jnp.minimum(nk - 1, ((((q_hi >> chunk_log2) + 1) << chunk_log2) - 1) // tk)
    needed = ki <= last_k
    full = jnp.logical_and((k_hi >> chunk_log2) <= (q_lo >> chunk_log2), k_hi < kv_len)

    @pl.when(ki == 0)
    def _init():
        m_sc[...] = jnp.full_like(m_sc, NEG_INF)
        acc_sc[...] = jnp.zeros_like(acc_sc)

    def step(masked):
        if masked:
            rows = q_lo + lax.broadcasted_iota(jnp.int32, (tq, tk), 0)
            cols = k_lo + lax.broadcasted_iota(jnp.int32, (tq, tk), 1)
            ok = (cols >> chunk_log2) <= (rows >> chunk_log2)
            if l_pad != kv_len:
                ok = jnp.logical_and(ok, cols < kv_len)
        ext_lane = lax.broadcasted_iota(jnp.int32, (tk, LANE), 1)
        ones_col = jnp.where(ext_lane == 0, 1.0, 0.0).astype(BF16)
        for h in range(HEADS):
            hs = slice(h * HEAD_DIM, (h + 1) * HEAD_DIM)
            if fox:
                q = jnp.concatenate([q_ref[0, :, hs], qx_ref[0]], axis=-1)
                kx = kx_ref[0]
                kx = jnp.where(ext_lane // EXT_COLS == h, kx, jnp.zeros_like(kx))
                k = jnp.concatenate([k_ref[0, :, hs], kx], axis=-1)
            else:
                q = q_ref[0, :, 2 * h * LANE:2 * (h + 1) * LANE]
                k = jnp.concatenate([k_ref[0, :, hs], kx_ref[0]], axis=-1)
            s = lax.dot_general(q, k, (((1,), (1,)), ((), ())), preferred_element_type=F32)
            if masked:
                s = jnp.where(ok, s, NEG_INF)
            m_prev = m_sc[h]
            m_new = jnp.maximum(m_prev, jnp.max(s, axis=-1, keepdims=True))
            alpha = jnp.exp2(m_prev - m_new)
            p = jnp.exp2(s - jnp.tile(m_new, (1, tk // LANE)))
            v = jnp.concatenate([v_ref[0, :, hs], ones_col], axis=-1)
            acc_sc[h] = jnp.tile(alpha, (1, 2)) * acc_sc[h] + jnp.dot(
                p.astype(BF16), v, preferred_element_type=F32)
            m_sc[h] = m_new

    @pl.when(jnp.logical_and(needed, full))
    def _full():
        step(False)

    @pl.when(jnp.logical_and(needed, jnp.logical_not(full)))
    def _masked():
        step(True)

    @pl.when(ki == last_k)
    def _fin():
        for h in range(HEADS):
            hs = slice(h * HEAD_DIM, (h + 1) * HEAD_DIM)
            acc = acc_sc[h]
            o_ref[0, :, hs] = (acc[:, :HEAD_DIM] / acc[:, HEAD_DIM:HEAD_DIM + 1]).astype(o_ref.dtype)


def _attention(q, k, v, extra, *, fox, tq, tk, past, kv_len, v_col=0):
    b, sq, qcols = q.shape
    l_pad = k.shape[1]
    nq, nk = sq // tq, l_pad // tk
    chunk_log2 = 0 if fox else CHUNK_LOG2
    width = HEADS * HEAD_DIM

    def kmap(bi, qi, ki):
        if nk == 1:
            return 0
        q_hi = past + qi * tq + (tq - 1)
        last_k = jnp.minimum(nk - 1, ((((q_hi >> chunk_log2) + 1) << chunk_log2) - 1) // tk)
        return jnp.minimum(ki, last_k)

    q_spec = pl.BlockSpec((1, tq, qcols), lambda bi, qi, ki: (bi, qi, 0))
    k_spec = pl.BlockSpec((1, tk, width), lambda bi, qi, ki: (bi, kmap(bi, qi, ki), 0))
    v_spec = pl.BlockSpec((1, tk, width), lambda bi, qi, ki: (bi, kmap(bi, qi, ki), v_col))
    kx_spec = pl.BlockSpec((1, tk, LANE), lambda bi, qi, ki: (bi, kmap(bi, qi, ki), 0))
    if fox:
        q_blk_off = past // tq
        qx_spec = pl.BlockSpec((1, tq, LANE), lambda bi, qi, ki: (bi, q_blk_off + qi, 0))
        in_specs = [q_spec, qx_spec, k_spec, kx_spec, v_spec]
        operands = (q, extra[0], k, extra[1], v)
    else:
        in_specs = [q_spec, k_spec, kx_spec, v_spec]
        operands = (q, k, extra[0], v)
    return pl.pallas_call(
        functools.partial(_attn_kernel, fox=fox, tq=tq, tk=tk, nk=nk, past=past,
                          chunk_log2=chunk_log2, kv_len=kv_len, l_pad=l_pad),
        grid=(b, nq, nk),
        in_specs=in_specs,
        out_specs=pl.BlockSpec((1, tq, width), lambda bi, qi, ki: (bi, qi, 0)),
        out_shape=jax.ShapeDtypeStruct((b, sq, width), BF16),
        scratch_shapes=[pltpu.VMEM((HEADS, tq, LANE), F32),
                        pltpu.VMEM((HEADS, tq, 2 * LANE), F32)],
        compiler_params=_cparams(("parallel", "parallel", "arbitrary")),
        name="attn_fox" if fox else "attn_mla",
    )(*operands)


def _merge_kernel(om_ref, of_ref, wm_ref, wf_ref, gm_ref, gf_ref, o_ref):
    a = jnp.dot(om_ref[...], wm_ref[...], preferred_element_type=F32)
    b = jnp.dot(of_ref[...], wf_ref[...], preferred_element_type=F32)
    o_ref[...] = (gm_ref[...].astype(F32) * a + gf_ref[...].astype(F32) * b).astype(o_ref.dtype)


def _merge(o_mla, o_fox, w_o_mla, w_o_fox, gates, tm, tn):
    t, width = o_mla.shape
    ncol = D_MODEL // tn
    return pl.pallas_call(
        _merge_kernel,
        grid=(t // tm, ncol),
        in_specs=[pl.BlockSpec((tm, width), lambda i, j: (i, 0)),
                  pl.BlockSpec((tm, width), lambda i, j: (i, 0)),
                  pl.BlockSpec((width, tn), lambda i, j: (0, j)),
                  pl.BlockSpec((width, tn), lambda i, j: (0, j)),
                  pl.BlockSpec((tm, tn), lambda i, j: (i, j)),
                  pl.BlockSpec((tm, tn), lambda i, j: (i, ncol + j))],
        out_specs=pl.BlockSpec((tm, tn), lambda i, j: (i, j)),
        out_shape=jax.ShapeDtypeStruct((t, D_MODEL), BF16),
        compiler_params=_cparams(("parallel", "parallel")),
        name="merge",
    )(o_mla, o_fox, w_o_mla, w_o_fox, gates, gates)


def _outproj_kernel(x_ref, mg_ref, w_ref, g1_ref, n2_ref, sh_ref, sc_ref, x1_ref, h2_ref, h2t_ref):
    y = jnp.dot(mg_ref[...], w_ref[...], preferred_element_type=F32)
    x1 = x_ref[...] + _mod_rows(g1_ref) * y
    x1_ref[...] = x1
    n = x1 * lax.rsqrt(jnp.mean(x1 * x1, axis=-1, keepdims=True) + EPS) * n2_ref[...]
    h2 = n * (1.0 + _mod_rows(sc_ref)) + _mod_rows(sh_ref)
    h2_ref[...] = h2.astype(h2_ref.dtype)
    h2t_ref[...] = h2.T.astype(h2t_ref.dtype)


def _outproj(x, merged, w_out, gate1, norm2_g, shift2, scale2, seq, tm):
    t = x.shape[0]
    row = pl.BlockSpec((tm, D_MODEL), lambda i: (i, 0))
    return pl.pallas_call(
        _outproj_kernel,
        grid=(t // tm,),
        in_specs=[row, row,
                  pl.BlockSpec((D_MODEL, D_MODEL), lambda i: (0, 0)),
                  _mod_spec(gate1, tm, seq),
                  pl.BlockSpec((1, D_MODEL), lambda i: (0, 0)),
                  _mod_spec(shift2, tm, seq), _mod_spec(scale2, tm, seq)],
        out_specs=[row, row, pl.BlockSpec((D_MODEL, tm), lambda i: (0, i))],
        out_shape=[jax.ShapeDtypeStruct((t, D_MODEL), F32),
                   jax.ShapeDtypeStruct((t, D_MODEL), BF16),
                   jax.ShapeDtypeStruct((D_MODEL, t), BF16)],
        compiler_params=_cparams(("parallel",)),
        name="outproj",
    )(x, merged, w_out, gate1, norm2_g, shift2, scale2)


def _peer_score_kernel(h_ref, wq_ref, k1_ref, k2_ref, s1_ref, s2_ref):
    q = jnp.dot(h_ref[...], wq_ref[...], preferred_element_type=F32).astype(BF16)
    nt = (((1,), (1,)), ((), ()))
    half = N_KEYS
    for h in range(HEADS):
        base = h * 2 * half
        s1_ref[h] = lax.dot_general(k1_ref[...], q[:, base:base + half], nt,
                                    preferred_element_type=F32)
        s2_ref[h] = lax.dot_general(k2_ref[...], q[:, base + half:base + 2 * half], nt,
                                    preferred_element_type=F32)


def _peer_scores(h2, wq, k1, k2, tm):
    t = h2.shape[0]
    out = jax.ShapeDtypeStruct((HEADS, N_KEYS, t), F32)
    ospec = pl.BlockSpec((HEADS, N_KEYS, tm), lambda i: (0, 0, i))
    kspec = pl.BlockSpec((N_KEYS, N_KEYS), lambda i: (0, 0))
    return pl.pallas_call(
        _peer_score_kernel,
        grid=(t // tm,),
        in_specs=[pl.BlockSpec((tm, D_MODEL), lambda i: (i, 0)),
                  pl.BlockSpec((D_MODEL, D_MODEL), lambda i: (0, 0)),
                  kspec, kspec],
        out_specs=[ospec, ospec],
        out_shape=[out, out],
        compiler_params=_cparams(("parallel",)),
        name="peer_scores",
    )(h2, wq, k1, k2)


_N_RANK = PEER_TOPK + 1
_CAND_PAIRS = [(i, j) for i in range(_N_RANK) for j in range(_N_RANK) if (i + 1) * (j + 1) <= _N_RANK]
_CAND_ROWS = -(-len(_CAND_PAIRS) // 8) * 8


_NO_RANK = 127.0


def _top_values(x, n, ranked=False):
    vals = []
    rank = jnp.full(x.shape, _NO_RANK, F32) if ranked else None
    for r in range(n):
        mx = jnp.max(x, axis=0, keepdims=True)
        vals.append(mx)
        hit = x == mx
        if ranked:
            rank = jnp.where(hit, float(r), rank)
        x = jnp.where(hit, -jnp.inf, x)
    return (vals, rank) if ranked else vals


def _peer_topk_kernel(s1_ref, s2_ref, c1_ref, n1_ref, r2_ref, e2_ref, cand_sc):
    for h in range(HEADS):
        a = s1_ref[h]
        b = s2_ref[h]
        v1 = _top_values(a, _N_RANK)
        v2, rank2 = _top_values(b, _N_RANK, ranked=True)
        cand_sc[...] = jnp.full_like(cand_sc, -jnp.inf)
        for r, (i, j) in enumerate(_CAND_PAIRS):
            cand_sc[r:r + 1, :] = v1[i] + v2[j]
        tops = _top_values(cand_sc[...], _N_RANK)
        tau = 0.5 * (tops[PEER_TOPK - 1] + tops[PEER_TOPK])
        z = jnp.exp(tops[0] - tops[0])
        for t in tops[1:PEER_TOPK]:
            z = z + jnp.exp(t - tops[0])
        n1 = jnp.zeros_like(a)
        for j in range(PEER_TOPK):
            n1 = n1 + jnp.where(a >= tau - v2[j], 1.0, 0.0)
        c1_ref[h] = jnp.exp(a - v1[0]) / z
        n1_ref[h] = n1
        r2_ref[h] = rank2.astype(r2_ref.dtype)
        e2_ref[h] = jnp.exp(b - v2[0]).astype(e2_ref.dtype)


def _peer_topk(s1t, s2t, tl):
    t = s1t.shape[2]
    spec = pl.BlockSpec((HEADS, N_KEYS, tl), lambda i: (0, 0, i))
    f32 = jax.ShapeDtypeStruct(s1t.shape, F32)
    bf16 = jax.ShapeDtypeStruct(s1t.shape, BF16)
    return pl.pallas_call(
        _peer_topk_kernel,
        grid=(t // tl,),
        in_specs=[spec, spec],
        out_specs=[spec, spec, spec, spec],
        out_shape=[f32, f32, bf16, bf16],
        scratch_shapes=[pltpu.VMEM((_CAND_ROWS, tl), F32)],
        compiler_params=_cparams(("parallel",)),
        name="peer_topk",
    )(s1t, s2t)


def _peer_dense_kernel(ht_ref, u_ref, vt_ref, c1_ref, n1_ref, r2_ref, e2_ref, o_ref,
                       a_sc, w_sc, acc_sc):
    c = pl.program_id(1)

    @pl.when(c == 0)
    def _():
        acc_sc[...] = jnp.zeros_like(acc_sc)

    zero = jnp.zeros((), BF16)
    tb = ht_ref.shape[1]

    def row_bf16(ref, h, ii):
        tile = jnp.broadcast_to(ref[h, ii:ii + 1, :], (16, tb)).astype(BF16)
        return jnp.tile(tile, (N_KEYS // 16, 1))

    def activations(part):
        a_sc[part, :] = jnp.dot(u_ref[part, :], ht_ref[...], preferred_element_type=F32)

    def gates(ii):
        rows = slice(ii * N_KEYS, (ii + 1) * N_KEYS)
        g = None
        for h in range(HEADS):
            term = jnp.where(r2_ref[h] < row_bf16(n1_ref, h, ii), e2_ref[h] * row_bf16(c1_ref, h, ii), zero)
            g = term if g is None else g + term
        a = a_sc[rows, :]
        gelu = a * (lax.erf(a / math.sqrt(2.0)) + 1.0) / 2.0
        w_sc[rows, :] = g * gelu.astype(BF16)

    def outputs(part):
        acc_sc[...] += jnp.dot(vt_ref[:, part], w_sc[part, :], preferred_element_type=F32)

    n_act, n_out = ACT_SPLIT, OUT_SPLIT
    for q in range(n_act):
        activations(slice(q * EXPERT_CHUNK // n_act, (q + 1) * EXPERT_CHUNK // n_act))
    per_out = EXPERT_CHUNK // N_KEYS // n_out
    for q in range(n_out):
        for ii in range(q * per_out, (q + 1) * per_out):
            gates(ii)
        outputs(slice(q * EXPERT_CHUNK // n_out, (q + 1) * EXPERT_CHUNK // n_out))

    @pl.when(c == pl.num_programs(1) - 1)
    def _():
        o_ref[...] = acc_sc[...].T


def _peer_dense(h2t, u_bf, vt_bf, c1t, n1t, r2t, e2t, tb):
    t = h2t.shape[1]
    n_exp = u_bf.shape[0]
    rows = EXPERT_CHUNK // N_KEYS
    small = pl.BlockSpec((HEADS, rows, tb), lambda i, c: (0, c, i))
    big = pl.BlockSpec((HEADS, N_KEYS, tb), lambda i, c: (0, 0, i))
    return pl.pallas_call(
        _peer_dense_kernel,
        grid=(t // tb, n_exp // EXPERT_CHUNK),
        in_specs=[pl.BlockSpec((D_MODEL, tb), lambda i, c: (0, i)),
                  pl.BlockSpec((EXPERT_CHUNK, D_MODEL), lambda i, c: (c, 0)),
                  pl.BlockSpec((D_MODEL, EXPERT_CHUNK), lambda i, c: (0, c)),
                  small, small, big, big],
        out_specs=pl.BlockSpec((tb, D_MODEL), lambda i, c: (i, 0)),
        out_shape=jax.ShapeDtypeStruct((t, D_MODEL), F32),
        scratch_shapes=[pltpu.VMEM((EXPERT_CHUNK, tb), F32),
                        pltpu.VMEM((EXPERT_CHUNK, tb), BF16),
                        pltpu.VMEM((D_MODEL, tb), F32)],
        compiler_params=_cparams(("parallel", "arbitrary")),
        name="peer_dense",
    )(h2t, u_bf, vt_bf, c1t, n1t, r2t, e2t)


def _final_kernel(x_ref, p_ref, g2_ref, fg_ref, o_ref):
    x = x_ref[...] + _mod_rows(g2_ref) * p_ref[...]
    o_ref[...] = x * lax.rsqrt(jnp.mean(x * x, axis=-1, keepdims=True) + EPS) * fg_ref[...]


def _final(x1, peer, gate2, final_g, seq, tm):
    t = x1.shape[0]
    row = pl.BlockSpec((tm, D_MODEL), lambda i: (i, 0))
    return pl.pallas_call(
        _final_kernel,
        grid=(t // tm,),
        in_specs=[row, row, _mod_spec(gate2, tm, seq), pl.BlockSpec((1, D_MODEL), lambda i: (0, 0))],
        out_specs=row,
        out_shape=jax.ShapeDtypeStruct((t, D_MODEL), F32),
        compiler_params=_cparams(("parallel",)),
        name="final",
    )(x1, peer, gate2, final_g)


def _rope_tables(pos):
    inv = 1.0 / (ROPE_THETA ** (jnp.arange(0, QK_ROPE, 2, dtype=F32) / QK_ROPE))
    ang = pos.astype(F32)[:, None] * inv[None, :]
    cos, sin = jnp.cos(ang), jnp.sin(ang)
    z = jnp.zeros_like(cos)
    return (jnp.concatenate([cos, cos, z, z], axis=1),
            jnp.concatenate([-sin, z, z, z], axis=1),
            jnp.concatenate([z, sin, z, z], axis=1))


def _prep_weights(w_in, fox_fbias, w_ukv, w_o_mla, w_o_fox, w_out, peer_wq, peer_keys1, peer_keys2,
                  peer_u, peer_v):
    d = D_MODEL
    n_q = HEADS * (HEAD_DIM + QK_ROPE)
    wq = w_in[:, :n_q].reshape(d, HEADS, HEAD_DIM + QK_ROPE)
    wq_cat = jnp.concatenate([wq, jnp.zeros((d, HEADS, QK_ROPE), F32)], axis=-1).reshape(d, HEADS * 2 * LANE)
    o1 = n_q + KV_LORA
    o2 = o1 + QK_ROPE
    o3 = o2 + 3 * HEADS * HEAD_DIM
    o4 = o3 + HEADS
    w_kv = jnp.concatenate([w_in[:, n_q:o2], jnp.zeros((d, LANE - QK_ROPE), F32),
                            w_in[:, o3:o4], jnp.zeros((d, LANE - HEADS), F32)], axis=1)
    fb = jnp.concatenate([fox_fbias, jnp.zeros((LANE - HEADS,), F32)])[None, :]
    w_ukv3 = w_ukv.reshape(KV_LORA, HEADS, 2 * HEAD_DIM)
    w_ukv_r = jnp.concatenate([w_ukv3[:, :, :HEAD_DIM].reshape(KV_LORA, -1),
                               w_ukv3[:, :, HEAD_DIM:].reshape(KV_LORA, -1)], axis=1)
    return dict(
        wq_cat=wq_cat.astype(BF16), w_kv=w_kv.astype(BF16), fb=fb,
        w_fox=w_in[:, o2:o3].astype(BF16), w_gate=w_in[:, o4:].astype(BF16),
        w_ukv=w_ukv_r.astype(BF16), w_o_mla=w_o_mla.astype(BF16), w_o_fox=w_o_fox.astype(BF16),
        w_out=w_out.astype(BF16), peer_wq=peer_wq.astype(BF16),
        k1=peer_keys1.astype(BF16), k2=peer_keys2.astype(BF16),
        u=peer_u.astype(BF16), vt=peer_v.T.astype(BF16))


def _layer(x, mods, caches, wts, norm1_g, kv_norm_g, norm2_g, final_g, *, batch, seq, past, tm, tq, tk):
    t = batch * seq
    shift1, scale1, gate1, shift2, scale2, gate2 = mods
    width = HEADS * HEAD_DIM
    pos = past + jnp.arange(seq, dtype=jnp.int32)
    tabs = _rope_tables(pos)
    if tm > seq:
        tabs = tuple(jnp.tile(tb, (tm // seq, 1)) for tb in tabs)

    h = _norm_mod(x, norm1_g, shift1, scale1, seq, min(tm, 512))
    q_mla = _proj_qmla(h, wts["wq_cat"], tabs, tm, LOG2E * (HEAD_DIM + QK_ROPE) ** -0.5)
    lat, lat_b, k_r, k_rp, logf = _proj_kv(h, wts["w_kv"], kv_norm_g, wts["fb"], tabs, tm)
    w_fox = wts["w_fox"]
    (q_fox,) = _proj(h, w_fox[:, :width], [BF16], tm, width, scale=LOG2E * HEAD_DIM ** -0.5,
                      name="proj_foxq")
    k_new, k_new_b = _proj(h, w_fox[:, width:2 * width], [F32, BF16], tm, width, name="proj_foxk")
    v_new, v_new_b = _proj(h, w_fox[:, 2 * width:], [F32, BF16], tm, width, name="proj_foxv")
    (gates,) = _proj(h, wts["w_gate"], [BF16], tm, 1024, act="sigmoid", name="proj_gate")

    kv_len = past + seq
    l_pad = -(-kv_len // tk) * tk

    def with_cache(cache, new, cols):
        new = new.reshape(batch, seq, cols)
        if cache is None:
            return new
        parts = [cache.reshape(batch, past, cols).astype(new.dtype), new]
        if l_pad > kv_len:
            parts.append(jnp.zeros((batch, l_pad - kv_len, cols), new.dtype))
        return jnp.concatenate(parts, axis=1)

    c_lat, c_kr, c_k, c_v, c_logf = caches if caches is not None else (None,) * 5
    lat_all = with_cache(c_lat, lat_b, KV_LORA)
    kr_all = with_cache(None if c_kr is None else jnp.pad(c_kr, ((0, 0), (0, 0), (0, LANE - QK_ROPE))),
                        k_rp, LANE)
    if c_k is None:
        k_all, v_all = with_cache(None, k_new_b, width), with_cache(None, v_new_b, width)
    else:
        k_all, v_all = _pack_cache(c_k, c_v, k_new_b.reshape(batch, seq, width),
                                   v_new_b.reshape(batch, seq, width), l_pad)
    logf_all = with_cache(c_logf, logf, HEADS)

    (kv_up,) = _proj(lat_all.reshape(batch * l_pad, KV_LORA), wts["w_ukv"], [BF16],
                     min(1024, batch * l_pad), 2 * width, name="kv_up")
    kv_up = kv_up.reshape(batch, l_pad, 2 * width)
    o_mla = _attention(q_mla.reshape(batch, seq, -1), kv_up, kv_up, (kr_all,), fox=False,
                       tq=tq, tk=tk, past=past, kv_len=kv_len, v_col=1)

    q_ext, k_ext = _cumsum_ext(logf_all, tk)
    o_fox = _attention(q_fox.reshape(batch, seq, width), k_all, v_all, (q_ext, k_ext), fox=True,
                       tq=tq, tk=tk, past=past, kv_len=kv_len)

    merged = _merge(o_mla.reshape(t, width), o_fox.reshape(t, width), wts["w_o_mla"], wts["w_o_fox"],
                    gates, tm, 1024)
    tm2 = 512 if gate1.ndim == 3 else 256
    x1, h2, h2t = _outproj(x, merged, wts["w_out"], gate1, norm2_g, shift2, scale2, seq, tm2)

    s1t, s2t = _peer_scores(h2, wts["peer_wq"], wts["k1"], wts["k2"], tm2)
    c1t, n1t, r2t, e2t = _peer_topk(s1t, s2t, 256)
    peer = _peer_dense(h2t, wts["u"], wts["vt"], c1t, n1t, r2t, e2t, 512)
    y = _final(x1, peer, gate2, final_g, seq, tm2)
    return y, (lat, k_r, k_new, v_new, logf)


def kernel(x_prompt, x_sample, c_prompt, c_sample, cache_mla_latent, cache_mla_krope, cache_fox_k,
           cache_fox_v, cache_fox_logf, w_ada, b_ada, norm1_g, w_in, fox_fbias, kv_norm_g, w_ukv,
           w_o_mla, w_o_fox, w_out, norm2_g, peer_wq, peer_keys1, peer_keys2, peer_u, peer_v, final_g):
    depth = w_ada.shape[0]
    assert depth == 1, "the final RMSNorm is fused into the single trunk layer"
    bp, sp, d = x_prompt.shape
    bs, ss, _ = x_sample.shape
    past = cache_mla_latent.shape[2]
    xp = x_prompt.reshape(bp * sp, d)
    xs = x_sample.reshape(bs * ss, d)
    c_rows = -(-(bp + bs) // 16) * 16
    c_all = jnp.concatenate([c_prompt, c_sample, jnp.zeros((c_rows - bp - bs, d), F32)], axis=0)
    fg = final_g[None, :]
    states_p, states_s = [], []
    for l in range(depth):
        wts = _prep_weights(w_in[l], fox_fbias[l], w_ukv[l], w_o_mla[l], w_o_fox[l], w_out[l],
                            peer_wq[l], peer_keys1[l], peer_keys2[l], peer_u[l], peer_v[l])
        mod = _ada(c_all, w_ada[l], b_ada[l][None, :])
        mods_p = tuple(m[:, None, :] for m in jnp.split(mod[:bp], N_MOD, axis=-1))
        mods_s = tuple(jnp.repeat(m, ss, axis=0) for m in jnp.split(mod[bp:bp + bs], N_MOD, axis=-1))
        n1, kvg, n2 = norm1_g[l][None, :], kv_norm_g[l][None, :], norm2_g[l][None, :]
        xp, st_p = _layer(xp, mods_p, None, wts, n1, kvg, n2, fg,
                          batch=bp, seq=sp, past=0, tm=1024, tq=512, tk=512)
        caches = (cache_mla_latent[l], cache_mla_krope[l], cache_fox_k[l], cache_fox_v[l], cache_fox_logf[l])
        xs, st_s = _layer(xs, mods_s, caches, wts, n1, kvg, n2, fg,
                          batch=bs, seq=ss, past=past, tm=1024, tq=ss, tk=1152)
        states_p.append(st_p)
        states_s.append(st_s)

    def stack(states, i, batch, seq, tail):
        return jnp.stack([st[i].reshape((batch, seq) + tail) for st in states])

    tails = [(KV_LORA,), (QK_ROPE,), (HEADS, HEAD_DIM), (HEADS, HEAD_DIM), (HEADS,)]
    outs_p = [stack(states_p, i, bp, sp, tails[i]) for i in range(5)]
    outs_s = [stack(states_s, i, bs, ss, tails[i]) for i in range(5)]
    return (xp.reshape(bp, sp, d), xs.reshape(bs, ss, d), *outs_p, *outs_s)
```

```python
import functools
import math

import jax
import jax.numpy as jnp
import numpy as np
from jax import lax
from jax.experimental import pallas as pl
from jax.experimental.pallas import tpu as pltpu

F32 = jnp.float32
BF16 = jnp.bfloat16

D_MODEL = 2048
EPS = 1e-6
NEG_INF = -1e30
LOG2E = 1.4426950408889634
N_MOD = 6
HEADS = 8
HEAD_DIM = 128
QK_ROPE = 64
KV_LORA = 512
CHUNK_LOG2 = 6
ROPE_THETA = 10000.0
N_KEYS = 128
PEER_TOPK = 16
LANE = 128
EXT_COLS = 8
EXPERT_CHUNK = 1024
ACT_SPLIT = 4
OUT_SPLIT = 2
VMEM_LIMIT_MB = 56


def _cparams(sem, vmem_mb=VMEM_LIMIT_MB, flags=None):
    return pltpu.CompilerParams(dimension_semantics=sem, vmem_limit_bytes=vmem_mb * 1024 * 1024,
                                flags=flags)


def _mod_rows(ref):
    v = ref[...]
    return v[0] if v.ndim == 3 else v


def _mod_spec(mod, tm, seq):
    if mod.ndim == 3:
        blocks_per_batch = seq // tm
        return pl.BlockSpec((1, 1, D_MODEL), lambda i, *_: (i // blocks_per_batch, 0, 0))
    return pl.BlockSpec((tm, D_MODEL), lambda i, *_: (i, 0))


def _rope128(r, c, t1, t2):
    return r * c + pltpu.roll(r, 96, 1) * t1 + pltpu.roll(r, 32, 1) * t2


def _ada_kernel(c_ref, w_ref, b_ref, o_ref):
    c = c_ref[...]
    a = (c * jax.nn.sigmoid(c)).astype(BF16)
    o_ref[...] = jnp.dot(a, w_ref[...].astype(BF16), preferred_element_type=F32) + b_ref[...]


def _ada(c_all, w_ada, b_ada):
    rows, n = c_all.shape[0], w_ada.shape[1]
    tn = 1024
    return pl.pallas_call(
        _ada_kernel,
        grid=(n // tn,),
        in_specs=[pl.BlockSpec((rows, D_MODEL), lambda j: (0, 0)),
                  pl.BlockSpec((D_MODEL, tn), lambda j: (0, j)),
                  pl.BlockSpec((1, tn), lambda j: (0, j))],
        out_specs=pl.BlockSpec((rows, tn), lambda j: (0, j)),
        out_shape=jax.ShapeDtypeStruct((rows, n), F32),
        compiler_params=_cparams(("parallel",)),
        name="ada",
    )(c_all, w_ada, b_ada)


def _norm_mod_kernel(x_ref, g_ref, sh_ref, sc_ref, o_ref):
    x = x_ref[...]
    y = x * lax.rsqrt(jnp.mean(x * x, axis=-1, keepdims=True) + EPS) * g_ref[...]
    o_ref[...] = (y * (1.0 + _mod_rows(sc_ref)) + _mod_rows(sh_ref)).astype(o_ref.dtype)


def _norm_mod(x, g, shift, scale, seq, tm):
    t = x.shape[0]
    return pl.pallas_call(
        _norm_mod_kernel,
        grid=(t // tm,),
        in_specs=[pl.BlockSpec((tm, D_MODEL), lambda i: (i, 0)),
                  pl.BlockSpec((1, D_MODEL), lambda i: (0, 0)),
                  _mod_spec(shift, tm, seq), _mod_spec(scale, tm, seq)],
        out_specs=pl.BlockSpec((tm, D_MODEL), lambda i: (i, 0)),
        out_shape=jax.ShapeDtypeStruct((t, D_MODEL), BF16),
        compiler_params=_cparams(("parallel",)),
        name="norm_mod",
    )(x, g, shift, scale)


def _proj_kernel(h_ref, w_ref, *o_refs, scale, act):
    acc = jnp.dot(h_ref[...], w_ref[...], preferred_element_type=F32)
    if scale != 1.0:
        acc = acc * scale
    if act == "sigmoid":
        acc = jax.nn.sigmoid(acc)
    for o_ref in o_refs:
        o_ref[...] = acc.astype(o_ref.dtype)


def _proj(h, w, out_dtypes, tm, tn, scale=1.0, act=None, name="proj"):
    t, k = h.shape
    n = w.shape[1]
    outs = pl.pallas_call(
        functools.partial(_proj_kernel, scale=scale, act=act),
        grid=(t // tm, n // tn),
        in_specs=[pl.BlockSpec((tm, k), lambda i, j: (i, 0)),
                  pl.BlockSpec((k, tn), lambda i, j: (0, j))],
        out_specs=[pl.BlockSpec((tm, tn), lambda i, j: (i, j)) for _ in out_dtypes],
        out_shape=[jax.ShapeDtypeStruct((t, n), dt) for dt in out_dtypes],
        compiler_params=_cparams(("parallel", "parallel")),
        name=name,
    )(h, w)
    return outs


def _proj_qmla_kernel(h_ref, w_ref, c_ref, t1_ref, t2_ref, o_ref, *, scale):
    acc = jnp.dot(h_ref[...], w_ref[...], preferred_element_type=F32) * scale
    for hh in range(acc.shape[1] // (2 * LANE)):
        base = hh * 2 * LANE
        rot = _rope128(acc[:, base + LANE:base + 2 * LANE], c_ref[...], t1_ref[...], t2_ref[...])
        o_ref[:, base:base + LANE] = acc[:, base:base + LANE].astype(o_ref.dtype)
        o_ref[:, base + LANE:base + 2 * LANE] = rot.astype(o_ref.dtype)


def _proj_qmla(h, w, tabs, tm, scale, heads_per_step=4):
    t = h.shape[0]
    ntab = tabs[0].shape[0] // tm
    tn = heads_per_step * 2 * LANE
    tab_spec = pl.BlockSpec((tm, LANE), lambda i, j: (i % ntab, 0))
    return pl.pallas_call(
        functools.partial(_proj_qmla_kernel, scale=scale),
        grid=(t // tm, HEADS // heads_per_step),
        in_specs=[pl.BlockSpec((tm, D_MODEL), lambda i, j: (i, 0)),
                  pl.BlockSpec((D_MODEL, tn), lambda i, j: (0, j)),
                  tab_spec, tab_spec, tab_spec],
        out_specs=pl.BlockSpec((tm, tn), lambda i, j: (i, j)),
        out_shape=jax.ShapeDtypeStruct((t, HEADS * 2 * LANE), BF16),
        compiler_params=_cparams(("parallel", "parallel")),
        name="proj_qmla",
    )(h, w, *tabs)


def _proj_kv_kernel(h_ref, w_ref, g_ref, fb_ref, c_ref, t1_ref, t2_ref,
                    lat_ref, latb_ref, kr_ref, krp_ref, logf_ref):
    acc = jnp.dot(h_ref[...], w_ref[...], preferred_element_type=F32)
    ckv = acc[:, :KV_LORA]
    lat = ckv * lax.rsqrt(jnp.mean(ckv * ckv, axis=-1, keepdims=True) + EPS) * g_ref[...]
    lat_ref[...] = lat
    latb_ref[...] = lat.astype(latb_ref.dtype)
    rot = _rope128(acc[:, KV_LORA:KV_LORA + LANE], c_ref[...], t1_ref[...], t2_ref[...])
    kr_ref[...] = rot[:, :QK_ROPE]
    krp_ref[...] = rot.astype(krp_ref.dtype)
    f = acc[:, KV_LORA + LANE:] + fb_ref[...]
    logf = jnp.minimum(f, 0.0) - jnp.log1p(jnp.exp(-jnp.abs(f)))
    logf_ref[...] = logf[:, :HEADS]


def _proj_kv(h, w, g, fb, tabs, tm):
    t = h.shape[0]
    n = w.shape[1]
    ntab = tabs[0].shape[0] // tm
    tab_spec = pl.BlockSpec((tm, LANE), lambda i: (i % ntab, 0))
    row = lambda width: pl.BlockSpec((tm, width), lambda i: (i, 0))
    return pl.pallas_call(
        _proj_kv_kernel,
        grid=(t // tm,),
        in_specs=[row(D_MODEL),
                  pl.BlockSpec((D_MODEL, n), lambda i: (0, 0)),
                  pl.BlockSpec((1, KV_LORA), lambda i: (0, 0)),
                  pl.BlockSpec((1, LANE), lambda i: (0, 0)),
                  tab_spec, tab_spec, tab_spec],
        out_specs=[row(KV_LORA), row(KV_LORA), row(QK_ROPE), row(LANE), row(HEADS)],
        out_shape=[jax.ShapeDtypeStruct((t, KV_LORA), F32),
                   jax.ShapeDtypeStruct((t, KV_LORA), BF16),
                   jax.ShapeDtypeStruct((t, QK_ROPE), F32),
                   jax.ShapeDtypeStruct((t, LANE), BF16),
                   jax.ShapeDtypeStruct((t, HEADS), F32)],
        compiler_params=_cparams(("parallel",)),
        name="proj_kv",
    )(h, w, g, fb, *tabs)


def _split3(x):
    hi = x.astype(BF16).astype(F32)
    r1 = x - hi
    mid = r1.astype(BF16).astype(F32)
    lo = (r1 - mid).astype(BF16).astype(F32)
    return hi, mid, lo


def _cumsum_kernel(x_ref, tril_ref, pq_ref, pk_ref, oq_ref, ok_ref, qa_ref, ka_ref, carry_ref, *, blk):
    @pl.when(pl.program_id(1) == 0)
    def _():
        carry_ref[...] = jnp.zeros_like(carry_ref)

    hi, mid, lo = _split3(x_ref[0])
    tril = tril_ref[...]
    carry = carry_ref[...]
    parts = []
    for s in range(blk // LANE):
        rows = slice(s * LANE, (s + 1) * LANE)
        part = (jnp.dot(tril, lo[rows].astype(BF16), preferred_element_type=F32)
                + jnp.dot(tril, mid[rows].astype(BF16), preferred_element_type=F32)
                + jnp.dot(tril, hi[rows].astype(BF16), preferred_element_type=F32)) + carry
        carry = part[LANE - 1:LANE, :]
        parts.append(part)
    y = jnp.concatenate(parts, axis=0)
    carry_ref[...] = carry
    pieces = _split3(y * LOG2E)
    qa = oq_ref[...]
    ka = ok_ref[...]
    for j, piece in enumerate(pieces):
        piece = piece.astype(BF16)
        qa = qa + jnp.dot(piece, pq_ref[j], preferred_element_type=F32)
        ka = ka - jnp.dot(piece, pk_ref[j], preferred_element_type=F32)
    qa_ref[0] = qa.astype(qa_ref.dtype)
    ka_ref[0] = ka.astype(ka_ref.dtype)


def _ext_placement():
    pq = np.zeros((3, HEADS, LANE), np.float32)
    pk = np.zeros((3, HEADS, LANE), np.float32)
    oq = np.zeros((1, LANE), np.float32)
    ok = np.zeros((1, LANE), np.float32)
    for h in range(HEADS):
        for j in range(3):
            pq[j, h, h * EXT_COLS + j] = 1.0
            pk[j, h, h * EXT_COLS + 3 + j] = 1.0
            oq[0, h * EXT_COLS + 3 + j] = 1.0
            ok[0, h * EXT_COLS + j] = 1.0
    return jnp.asarray(pq, BF16), jnp.asarray(pk, BF16), jnp.asarray(oq), jnp.asarray(ok)


def _cumsum_ext(logf, blk):
    b, l, h = logf.shape
    width = LANE
    pq, pk, oq, ok = _ext_placement()
    pspec = pl.BlockSpec((3, HEADS, width), lambda bi, i: (0, 0, 0))
    ospec = pl.BlockSpec((1, width), lambda bi, i: (0, 0))
    out = jax.ShapeDtypeStruct((b, l, width), BF16)
    tril = jnp.asarray(np.tril(np.ones((LANE, LANE), np.float32)), BF16)
    return pl.pallas_call(
        functools.partial(_cumsum_kernel, blk=blk),
        grid=(b, l // blk),
        in_specs=[pl.BlockSpec((1, blk, h), lambda bi, i: (bi, i, 0)),
                  pl.BlockSpec((LANE, LANE), lambda bi, i: (0, 0)), pspec, pspec, ospec, ospec],
        out_specs=[pl.BlockSpec((1, blk, width), lambda bi, i: (bi, i, 0))] * 2,
        out_shape=[out, out],
        scratch_shapes=[pltpu.VMEM((1, h), F32)],
        compiler_params=_cparams(("parallel", "arbitrary")),
        name="cumsum",
    )(logf, tril, pq, pk, oq, ok)


def _pack_cache_kernel(ck_ref, cv_ref, nk_ref, nv_ref, ok_ref, ov_ref, *, past, seq):
    for c_ref, n_ref, o_ref in ((ck_ref, nk_ref, ok_ref), (cv_ref, nv_ref, ov_ref)):
        by_head = jnp.swapaxes(c_ref[0], 0, 1)
        for h in range(HEADS):
            o_ref[0, :past, h * HEAD_DIM:(h + 1) * HEAD_DIM] = by_head[h].astype(o_ref.dtype)
        o_ref[0, past:past + seq, :] = n_ref[0]
        o_ref[0, past + seq:, :] = jnp.zeros((o_ref.shape[1] - past - seq, o_ref.shape[2]), o_ref.dtype)


def _pack_cache(cache_k, cache_v, new_k, new_v, l_pad):
    b, past = cache_k.shape[:2]
    seq, width = new_k.shape[1:]
    cspec = pl.BlockSpec((1, past, HEADS, HEAD_DIM), lambda bi: (bi, 0, 0, 0))
    nspec = pl.BlockSpec((1, seq, width), lambda bi: (bi, 0, 0))
    ospec = pl.BlockSpec((1, l_pad, width), lambda bi: (bi, 0, 0))
    out = jax.ShapeDtypeStruct((b, l_pad, width), BF16)
    return pl.pallas_call(
        functools.partial(_pack_cache_kernel, past=past, seq=seq),
        grid=(b,),
        in_specs=[cspec, cspec, nspec, nspec],
        out_specs=[ospec, ospec],
        out_shape=[out, out],
        compiler_params=_cparams(("parallel",)),
        name="pack_cache",
    )(cache_k, cache_v, new_k, new_v)


def _attn_kernel(*refs, fox, tq, tk, nk, past, chunk_log2, kv_len, l_pad):
    if fox:
        q_ref, qx_ref, k_ref, kx_ref, v_ref, o_ref, m_sc, acc_sc = refs
    else:
        q_ref, k_ref, kx_ref, v_ref, o_ref, m_sc, acc_sc = refs
    qi = pl.program_id(1)
    ki = pl.program_id(2)
    q_lo = past + qi * tq
    q_hi = q_lo + (tq - 1)
    k_lo = ki * tk
    k_hi = k_lo + (tk - 1)
    if nk == 1:
        last_k = 0
    else:
        last_k = jnp.minimum(nk - 1, ((((q_hi >> chunk_log2) + 1) << chunk_log2) - 1) // tk)
    needed = ki <= last_k
    full = jnp.logical_and((k_hi >> chunk_log2) <= (q_lo >> chunk_log2), k_hi < kv_len)

    @pl.when(ki == 0)
    def _init():
        m_sc[...] = jnp.full_like(m_sc, NEG_INF)
        acc_sc[...] = jnp.zeros_like(acc_sc)

    def step(masked):
        if masked:
            rows = q_lo + lax.broadcasted_iota(jnp.int32, (tq, tk), 0)
            cols = k_lo + lax.broadcasted_iota(jnp.int32, (tq, tk), 1)
            ok = (cols >> chunk_log2) <= (rows >> chunk_log2)
            if l_pad != kv_len:
                ok = jnp.logical_and(ok, cols < kv_len)
        ext_lane = lax.broadcasted_iota(jnp.int32, (tk, LANE), 1)
        ones_col = jnp.where(ext_lane == 0, 1.0, 0.0).astype(BF16)
        for h in range(HEADS):
            hs = slice(h * HEAD_DIM, (h + 1) * HEAD_DIM)
            if fox:
                q = jnp.concatenate([q_ref[0, :, hs], qx_ref[0]], axis=-1)
                kx = kx_ref[0]
                kx = jnp.where(ext_lane // EXT_COLS == h, kx, jnp.zeros_like(kx))
                k = jnp.concatenate([k_ref[0, :, hs], kx], axis=-1)
            else:
                q = q_ref[0, :, 2 * h * LANE:2 * (h + 1) * LANE]
                k = jnp.concatenate([k_ref[0, :, hs], kx_ref[0]], axis=-1)
            s = lax.dot_general(q, k, (((1,), (1,)), ((), ())), preferred_element_type=F32)
            if masked:
                s = jnp.where(ok, s, NEG_INF)
            m_prev = m_sc[h]
            m_new = jnp.maximum(m_prev, jnp.max(s, axis=-1, keepdims=True))
            alpha = jnp.exp2(m_prev - m_new)
            p = jnp.exp2(s - jnp.tile(m_new, (1, tk // LANE)))
            v = jnp.concatenate([v_ref[0, :, hs], ones_col], axis=-1)
            acc_sc[h] = jnp.tile(alpha, (1, 2)) * acc_sc[h] + jnp.dot(
                p.astype(BF16), v, preferred_element_type=F32)
            m_sc[h] = m_new

    @pl.when(jnp.logical_and(needed, full))
    def _full():
        step(False)

    @pl.when(jnp.logical_and(needed, jnp.logical_not(full)))
    def _masked():
        step(True)

    @pl.when(ki == last_k)
    def _fin():
        for h in range(HEADS):
            hs = slice(h * HEAD_DIM, (h + 1) * HEAD_DIM)
            acc = acc_sc[h]
            o_ref[0, :, hs] = (acc[:, :HEAD_DIM] / acc[:, HEAD_DIM:HEAD_DIM + 1]).astype(o_ref.dtype)


def _attention(q, k, v, extra, *, fox, tq, tk, past, kv_len, v_col=0):
    b, sq, qcols = q.shape
    l_pad = k.shape[1]
    nq, nk = sq // tq, l_pad // tk
    chunk_log2 = 0 if fox else CHUNK_LOG2
    width = HEADS * HEAD_DIM

    def kmap(bi, qi, ki):
        if nk == 1:
            return 0
        q_hi = past + qi * tq + (tq - 1)
        last_k = jnp.minimum(nk - 1, ((((q_hi >> chunk_log2) + 1) << chunk_log2) - 1) // tk)
        return jnp.minimum(ki, last_k)

    q_spec = pl.BlockSpec((1, tq, qcols), lambda bi, qi, ki: (bi, qi, 0))
    k_spec = pl.BlockSpec((1, tk, width), lambda bi, qi, ki: (bi, kmap(bi, qi, ki), 0))
    v_spec = pl.BlockSpec((1, tk, width), lambda bi, qi, ki: (bi, kmap(bi, qi, ki), v_col))
    kx_spec = pl.BlockSpec((1, tk, LANE), lambda bi, qi, ki: (bi, kmap(bi, qi, ki), 0))
    if fox:
        q_blk_off = past // tq
        qx_spec = pl.BlockSpec((1, tq, LANE), lambda bi, qi, ki: (bi, q_blk_off + qi, 0))
        in_specs = [q_spec, qx_spec, k_spec, kx_spec, v_spec]
        operands = (q, extra[0], k, extra[1], v)
    else:
        in_specs = [q_spec, k_spec, kx_spec, v_spec]
        operands = (q, k, extra[0], v)
    return pl.pallas_call(
        functools.partial(_attn_kernel, fox=fox, tq=tq, tk=tk, nk=nk, past=past,
                          chunk_log2=chunk_log2, kv_len=kv_len, l_pad=l_pad),
        grid=(b, nq, nk),
        in_specs=in_specs,
        out_specs=pl.BlockSpec((1, tq, width), lambda bi, qi, ki: (bi, qi, 0)),
        out_shape=jax.ShapeDtypeStruct((b, sq, width), BF16),
        scratch_shapes=[pltpu.VMEM((HEADS, tq, LANE), F32),
                        pltpu.VMEM((HEADS, tq, 2 * LANE), F32)],
        compiler_params=_cparams(("parallel", "parallel", "arbitrary")),
        name="attn_fox" if fox else "attn_mla",
    )(*operands)


def _merge_kernel(om_ref, of_ref, wm_ref, wf_ref, gm_ref, gf_ref, o_ref):
    a = jnp.dot(om_ref[...], wm_ref[...], preferred_element_type=F32)
    b = jnp.dot(of_ref[...], wf_ref[...], preferred_element_type=F32)
    o_ref[...] = (gm_ref[...].astype(F32) * a + gf_ref[...].astype(F32) * b).astype(o_ref.dtype)


def _merge(o_mla, o_fox, w_o_mla, w_o_fox, gates, tm, tn):
    t, width = o_mla.shape
    ncol = D_MODEL // tn
    return pl.pallas_call(
        _merge_kernel,
        grid=(t // tm, ncol),
        in_specs=[pl.BlockSpec((tm, width), lambda i, j: (i, 0)),
                  pl.BlockSpec((tm, width), lambda i, j: (i, 0)),
                  pl.BlockSpec((width, tn), lambda i, j: (0, j)),
                  pl.BlockSpec((width, tn), lambda i, j: (0, j)),
                  pl.BlockSpec((tm, tn), lambda i, j: (i, j)),
                  pl.BlockSpec((tm, tn), lambda i, j: (i, ncol + j))],
        out_specs=pl.BlockSpec((tm, tn), lambda i, j: (i, j)),
        out_shape=jax.ShapeDtypeStruct((t, D_MODEL), BF16),
        compiler_params=_cparams(("parallel", "parallel")),
        name="merge",
    )(o_mla, o_fox, w_o_mla, w_o_fox, gates, gates)


def _outproj_kernel(x_ref, mg_ref, w_ref, g1_ref, n2_ref, sh_ref, sc_ref, x1_ref, h2_ref, h2t_ref):
    y = jnp.dot(mg_ref[...], w_ref[...], preferred_element_type=F32)
    x1 = x_ref[...] + _mod_rows(g1_ref) * y
    x1_ref[...] = x1
    n = x1 * lax.rsqrt(jnp.mean(x1 * x1, axis=-1, keepdims=True) + EPS) * n2_ref[...]
    h2 = n * (1.0 + _mod_rows(sc_ref)) + _mod_rows(sh_ref)
    h2_ref[...] = h2.astype(h2_ref.dtype)
    h2t_ref[...] = h2.T.astype(h2t_ref.dtype)


def _outproj(x, merged, w_out, gate1, norm2_g, shift2, scale2, seq, tm):
    t = x.shape[0]
    row = pl.BlockSpec((tm, D_MODEL), lambda i: (i, 0))
    return pl.pallas_call(
        _outproj_kernel,
        grid=(t // tm,),
        in_specs=[row, row,
                  pl.BlockSpec((D_MODEL, D_MODEL), lambda i: (0, 0)),
                  _mod_spec(gate1, tm, seq),
                  pl.BlockSpec((1, D_MODEL), lambda i: (0, 0)),
                  _mod_spec(shift2, tm, seq), _mod_spec(scale2, tm, seq)],
        out_specs=[row, row, pl.BlockSpec((D_MODEL, tm), lambda i: (0, i))],
        out_shape=[jax.ShapeDtypeStruct((t, D_MODEL), F32),
                   jax.ShapeDtypeStruct((t, D_MODEL), BF16),
                   jax.ShapeDtypeStruct((D_MODEL, t), BF16)],
        compiler_params=_cparams(("parallel",)),
        name="outproj",
    )(x, merged, w_out, gate1, norm2_g, shift2, scale2)


def _peer_score_kernel(h_ref, wq_ref, k1_ref, k2_ref, s1_ref, s2_ref):
    q = jnp.dot(h_ref[...], wq_ref[...], preferred_element_type=F32).astype(BF16)
    nt = (((1,), (1,)), ((), ()))
    half = N_KEYS
    for h in range(HEADS):
        base = h * 2 * half
        s1_ref[h] = lax.dot_general(k1_ref[...], q[:, base:base + half], nt,
                                    preferred_element_type=F32)
        s2_ref[h] = lax.dot_general(k2_ref[...], q[:, base + half:base + 2 * half], nt,
                                    preferred_element_type=F32)


def _peer_scores(h2, wq, k1, k2, tm):
    t = h2.shape[0]
    out = jax.ShapeDtypeStruct((HEADS, N_KEYS, t), F32)
    ospec = pl.BlockSpec((HEADS, N_KEYS, tm), lambda i: (0, 0, i))
    kspec = pl.BlockSpec((N_KEYS, N_KEYS), lambda i: (0, 0))
    return pl.pallas_call(
        _peer_score_kernel,
        grid=(t // tm,),
        in_specs=[pl.BlockSpec((tm, D_MODEL), lambda i: (i, 0)),
                  pl.BlockSpec((D_MODEL, D_MODEL), lambda i: (0, 0)),
                  kspec, kspec],
        out_specs=[ospec, ospec],
        out_shape=[out, out],
        compiler_params=_cparams(("parallel",)),
        name="peer_scores",
    )(h2, wq, k1, k2)


_N_RANK = PEER_TOPK + 1
_CAND_PAIRS = [(i, j) for i in range(_N_RANK) for j in range(_N_RANK) if (i + 1) * (j + 1) <= _N_RANK]
_CAND_ROWS = -(-len(_CAND_PAIRS) // 8) * 8


_NO_RANK = 127.0


def _top_values(x, n, ranked=False):
    vals = []
    rank = jnp.full(x.shape, _NO_RANK, F32) if ranked else None
    for r in range(n):
        mx = jnp.max(x, axis=0, keepdims=True)
        vals.append(mx)
        hit = x == mx
        if ranked:
            rank = jnp.where(hit, float(r), rank)
        x = jnp.where(hit, -jnp.inf, x)
    return (vals, rank) if ranked else vals


def _peer_topk_kernel(s1_ref, s2_ref, c1_ref, n1_ref, r2_ref, e2_ref, cand_sc):
    for h in range(HEADS):
        a = s1_ref[h]
        b = s2_ref[h]
        v1 = _top_values(a, _N_RANK)
        v2, rank2 = _top_values(b, _N_RANK, ranked=True)
        cand_sc[...] = jnp.full_like(cand_sc, -jnp.inf)
        for r, (i, j) in enumerate(_CAND_PAIRS):
            cand_sc[r:r + 1, :] = v1[i] + v2[j]
        tops = _top_values(cand_sc[...], _N_RANK)
        tau = 0.5 * (tops[PEER_TOPK - 1] + tops[PEER_TOPK])
        z = jnp.exp(tops[0] - tops[0])
        for t in tops[1:PEER_TOPK]:
            z = z + jnp.exp(t - tops[0])
        n1 = jnp.zeros_like(a)
        for j in range(PEER_TOPK):
            n1 = jnp.where(a >= tau - v2[j], float(j + 1), n1)
        c1_ref[h] = jnp.exp(a - v1[0]) / z
        n1_ref[h] = n1
        r2_ref[h] = rank2.astype(r2_ref.dtype)
        e2_ref[h] = jnp.exp(b - v2[0]).astype(e2_ref.dtype)


def _peer_topk(s1t, s2t, tl):
    t = s1t.shape[2]
    spec = pl.BlockSpec((HEADS, N_KEYS, tl), lambda i: (0, 0, i))
    f32 = jax.ShapeDtypeStruct(s1t.shape, F32)
    bf16 = jax.ShapeDtypeStruct(s1t.shape, BF16)
    return pl.pallas_call(
        _peer_topk_kernel,
        grid=(t // tl,),
        in_specs=[spec, spec],
        out_specs=[spec, spec, spec, spec],
        out_shape=[f32, f32, bf16, bf16],
        scratch_shapes=[pltpu.VMEM((_CAND_ROWS, tl), F32)],
        compiler_params=_cparams(("parallel",)),
        name="peer_topk",
    )(s1t, s2t)


def _peer_dense_kernel(ht_ref, u_ref, vt_ref, c1_ref, n1_ref, r2_ref, e2_ref, o_ref,
                       a_sc, w_sc, acc_sc):
    c = pl.program_id(1)

    @pl.when(c == 0)
    def _():
        acc_sc[...] = jnp.zeros_like(acc_sc)

    zero = jnp.zeros((), BF16)
    tb = ht_ref.shape[1]

    def row_bf16(ref, h, ii):
        tile = jnp.broadcast_to(ref[h, ii:ii + 1, :], (16, tb)).astype(BF16)
        return jnp.tile(tile, (N_KEYS // 16, 1))

    def activations(part):
        a_sc[part, :] = jnp.dot(u_ref[part, :], ht_ref[...], preferred_element_type=F32)

    def gates(ii):
        rows = slice(ii * N_KEYS, (ii + 1) * N_KEYS)
        g = None
        for h in range(HEADS):
            term = jnp.where(r2_ref[h] < row_bf16(n1_ref, h, ii), e2_ref[h] * row_bf16(c1_ref, h, ii), zero)
            g = term if g is None else g + term
        a = a_sc[rows, :]
        gelu = a * (lax.erf(a / math.sqrt(2.0)) + 1.0) / 2.0
        w_sc[rows, :] = g * gelu.astype(BF16)

    def outputs(part):
        acc_sc[...] += jnp.dot(vt_ref[:, part], w_sc[part, :], preferred_element_type=F32)

    n_act, n_out = ACT_SPLIT, OUT_SPLIT
    for q in range(n_act):
        activations(slice(q * EXPERT_CHUNK // n_act, (q + 1) * EXPERT_CHUNK // n_act))
    per_out = EXPERT_CHUNK // N_KEYS // n_out
    for q in range(n_out):
        for ii in range(q * per_out, (q + 1) * per_out):
            gates(ii)
        outputs(slice(q * EXPERT_CHUNK // n_out, (q + 1) * EXPERT_CHUNK // n_out))

    @pl.when(c == pl.num_programs(1) - 1)
    def _():
        o_ref[...] = acc_sc[...].T


def _peer_dense(h2t, u_bf, vt_bf, c1t, n1t, r2t, e2t, tb):
    t = h2t.shape[1]
    n_exp = u_bf.shape[0]
    rows = EXPERT_CHUNK // N_KEYS
    small = pl.BlockSpec((HEADS, rows, tb), lambda i, c: (0, c, i))
    big = pl.BlockSpec((HEADS, N_KEYS, tb), lambda i, c: (0, 0, i))
    return pl.pallas_call(
        _peer_dense_kernel,
        grid=(t // tb, n_exp // EXPERT_CHUNK),
        in_specs=[pl.BlockSpec((D_MODEL, tb), lambda i, c: (0, i)),
                  pl.BlockSpec((EXPERT_CHUNK, D_MODEL), lambda i, c: (c, 0)),
                  pl.BlockSpec((D_MODEL, EXPERT_CHUNK), lambda i, c: (0, c)),
                  small, small, big, big],
        out_specs=pl.BlockSpec((tb, D_MODEL), lambda i, c: (i, 0)),
        out_shape=jax.ShapeDtypeStruct((t, D_MODEL), F32),
        scratch_shapes=[pltpu.VMEM((EXPERT_CHUNK, tb), F32),
                        pltpu.VMEM((EXPERT_CHUNK, tb), BF16),
                        pltpu.VMEM((D_MODEL, tb), F32)],
        compiler_params=_cparams(("parallel", "arbitrary")),
        name="peer_dense",
    )(h2t, u_bf, vt_bf, c1t, n1t, r2t, e2t)


def _final_kernel(x_ref, p_ref, g2_ref, fg_ref, o_ref):
    x = x_ref[...] + _mod_rows(g2_ref) * p_ref[...]
    o_ref[...] = x * lax.rsqrt(jnp.mean(x * x, axis=-1, keepdims=True) + EPS) * fg_ref[...]


def _final(x1, peer, gate2, final_g, seq, tm):
    t = x1.shape[0]
    row = pl.BlockSpec((tm, D_MODEL), lambda i: (i, 0))
    return pl.pallas_call(
        _final_kernel,
        grid=(t // tm,),
        in_specs=[row, row, _mod_spec(gate2, tm, seq), pl.BlockSpec((1, D_MODEL), lambda i: (0, 0))],
        out_specs=row,
        out_shape=jax.ShapeDtypeStruct((t, D_MODEL), F32),
        compiler_params=_cparams(("parallel",)),
        name="final",
    )(x1, peer, gate2, final_g)


def _rope_tables(pos):
    inv = 1.0 / (ROPE_THETA ** (jnp.arange(0, QK_ROPE, 2, dtype=F32) / QK_ROPE))
    ang = pos.astype(F32)[:, None] * inv[None, :]
    cos, sin = jnp.cos(ang), jnp.sin(ang)
    z = jnp.zeros_like(cos)
    return (jnp.concatenate([cos, cos, z, z], axis=1),
            jnp.concatenate([-sin, z, z, z], axis=1),
            jnp.concatenate([z, sin, z, z], axis=1))


def _prep_weights(w_in, fox_fbias, w_ukv, w_o_mla, w_o_fox, w_out, peer_wq, peer_keys1, peer_keys2,
                  peer_u, peer_v):
    d = D_MODEL
    n_q = HEADS * (HEAD_DIM + QK_ROPE)
    wq = w_in[:, :n_q].reshape(d, HEADS, HEAD_DIM + QK_ROPE)
    wq_cat = jnp.concatenate([wq, jnp.zeros((d, HEADS, QK_ROPE), F32)], axis=-1).reshape(d, HEADS * 2 * LANE)
    o1 = n_q + KV_LORA
    o2 = o1 + QK_ROPE
    o3 = o2 + 3 * HEADS * HEAD_DIM
    o4 = o3 + HEADS
    w_kv = jnp.concatenate([w_in[:, n_q:o2], jnp.zeros((d, LANE - QK_ROPE), F32),
                            w_in[:, o3:o4], jnp.zeros((d, LANE - HEADS), F32)], axis=1)
    fb = jnp.concatenate([fox_fbias, jnp.zeros((LANE - HEADS,), F32)])[None, :]
    w_ukv3 = w_ukv.reshape(KV_LORA, HEADS, 2 * HEAD_DIM)
    w_ukv_r = jnp.concatenate([w_ukv3[:, :, :HEAD_DIM].reshape(KV_LORA, -1),
                               w_ukv3[:, :, HEAD_DIM:].reshape(KV_LORA, -1)], axis=1)
    return dict(
        wq_cat=wq_cat.astype(BF16), w_kv=w_kv.astype(BF16), fb=fb,
        w_fox=w_in[:, o2:o3].astype(BF16), w_gate=w_in[:, o4:].astype(BF16),
        w_ukv=w_ukv_r.astype(BF16), w_o_mla=w_o_mla.astype(BF16), w_o_fox=w_o_fox.astype(BF16),
        w_out=w_out.astype(BF16), peer_wq=peer_wq.astype(BF16),
        k1=peer_keys1.astype(BF16), k2=peer_keys2.astype(BF16),
        u=peer_u.astype(BF16), vt=peer_v.T.astype(BF16))


def _layer(x, mods, caches, wts, norm1_g, kv_norm_g, norm2_g, final_g, *, batch, seq, past, tm, tq, tk):
    t = batch * seq
    shift1, scale1, gate1, shift2, scale2, gate2 = mods
    width = HEADS * HEAD_DIM
    pos = past + jnp.arange(seq, dtype=jnp.int32)
    tabs = _rope_tables(pos)
    if tm > seq:
        tabs = tuple(jnp.tile(tb, (tm // seq, 1)) for tb in tabs)

    h = _norm_mod(x, norm1_g, shift1, scale1, seq, min(tm, 512))
    q_mla = _proj_qmla(h, wts["wq_cat"], tabs, tm, LOG2E * (HEAD_DIM + QK_ROPE) ** -0.5)
    lat, lat_b, k_r, k_rp, logf = _proj_kv(h, wts["w_kv"], kv_norm_g, wts["fb"], tabs, tm)
    w_fox = wts["w_fox"]
    (q_fox,) = _proj(h, w_fox[:, :width], [BF16], tm, width, scale=LOG2E * HEAD_DIM ** -0.5,
                      name="proj_foxq")
    k_new, k_new_b = _proj(h, w_fox[:, width:2 * width], [F32, BF16], tm, width, name="proj_foxk")
    v_new, v_new_b = _proj(h, w_fox[:, 2 * width:], [F32, BF16], tm, width, name="proj_foxv")
    (gates,) = _proj(h, wts["w_gate"], [BF16], tm, 1024, act="sigmoid", name="proj_gate")

    kv_len = past + seq
    l_pad = -(-kv_len // tk) * tk

    def with_cache(cache, new, cols):
        new = new.reshape(batch, seq, cols)
        if cache is None:
            return new
        parts = [cache.reshape(batch, past, cols).astype(new.dtype), new]
        if l_pad > kv_len:
            parts.append(jnp.zeros((batch, l_pad - kv_len, cols), new.dtype))
        return jnp.concatenate(parts, axis=1)

    c_lat, c_kr, c_k, c_v, c_logf = caches if caches is not None else (None,) * 5
    lat_all = with_cache(c_lat, lat_b, KV_LORA)
    kr_all = with_cache(None if c_kr is None else jnp.pad(c_kr, ((0, 0), (0, 0), (0, LANE - QK_ROPE))),
                        k_rp, LANE)
    if c_k is None:
        k_all, v_all = with_cache(None, k_new_b, width), with_cache(None, v_new_b, width)
    else:
        k_all, v_all = _pack_cache(c_k, c_v, k_new_b.reshape(batch, seq, width),
                                   v_new_b.reshape(batch, seq, width), l_pad)
    logf_all = with_cache(c_logf, logf, HEADS)

    (kv_up,) = _proj(lat_all.reshape(batch * l_pad, KV_LORA), wts["w_ukv"], [BF16],
                     min(1024, batch * l_pad), 2 * width, name="kv_up")
    kv_up = kv_up.reshape(batch, l_pad, 2 * width)
    o_mla = _attention(q_mla.reshape(batch, seq, -1), kv_up, kv_up, (kr_all,), fox=False,
                       tq=tq, tk=tk, past=past, kv_len=kv_len, v_col=1)

    q_ext, k_ext = _cumsum_ext(logf_all, tk)
    o_fox = _attention(q_fox.reshape(batch, seq, width), k_all, v_all, (q_ext, k_ext), fox=True,
                       tq=tq, tk=tk, past=past, kv_len=kv_len)

    merged = _merge(o_mla.reshape(t, width), o_fox.reshape(t, width), wts["w_o_mla"], wts["w_o_fox"],
                    gates, tm, 1024)
    tm2 = 512 if gate1.ndim == 3 else 256
    x1, h2, h2t = _outproj(x, merged, wts["w_out"], gate1, norm2_g, shift2, scale2, seq, tm2)

    s1t, s2t = _peer_scores(h2, wts["peer_wq"], wts["k1"], wts["k2"], tm2)
    c1t, n1t, r2t, e2t = _peer_topk(s1t, s2t, 256)
    peer = _peer_dense(h2t, wts["u"], wts["vt"], c1t, n1t, r2t, e2t, 512)
    y = _final(x1, peer, gate2, final_g, seq, tm2)
    return y, (lat, k_r, k_new, v_new, logf)


def kernel(x_prompt, x_sample, c_prompt, c_sample, cache_mla_latent, cache_mla_krope, cache_fox_k,
           cache_fox_v, cache_fox_logf, w_ada, b_ada, norm1_g, w_in, fox_fbias, kv_norm_g, w_ukv,
           w_o_mla, w_o_fox, w_out, norm2_g, peer_wq, peer_keys1, peer_keys2, peer_u, peer_v, final_g):
    depth = w_ada.shape[0]
    assert depth == 1, "the final RMSNorm is fused into the single trunk layer"
    bp, sp, d = x_prompt.shape
    bs, ss, _ = x_sample.shape
    past = cache_mla_latent.shape[2]
    xp = x_prompt.reshape(bp * sp, d)
    xs = x_sample.reshape(bs * ss, d)
    c_rows = -(-(bp + bs) // 16) * 16
    c_all = jnp.concatenate([c_prompt, c_sample, jnp.zeros((c_rows - bp - bs, d), F32)], axis=0)
    fg = final_g[None, :]
    states_p, states_s = [], []
    for l in range(depth):
        wts = _prep_weights(w_in[l], fox_fbias[l], w_ukv[l], w_o_mla[l], w_o_fox[l], w_out[l],
                            peer_wq[l], peer_keys1[l], peer_keys2[l], peer_u[l], peer_v[l])
        mod = _ada(c_all, w_ada[l], b_ada[l][None, :])
        mods_p = tuple(m[:, None, :] for m in jnp.split(mod[:bp], N_MOD, axis=-1))
        mods_s = tuple(jnp.repeat(m, ss, axis=0) for m in jnp.split(mod[bp:bp + bs], N_MOD, axis=-1))
        n1, kvg, n2 = norm1_g[l][None, :], kv_norm_g[l][None, :], norm2_g[l][None, :]
        xp, st_p = _layer(xp, mods_p, None, wts, n1, kvg, n2, fg,
                          batch=bp, seq=sp, past=0, tm=1024, tq=512, tk=512)
        caches = (cache_mla_latent[l], cache_mla_krope[l], cache_fox_k[l], cache_fox_v[l], cache_fox_logf[l])
        xs, st_s = _layer(xs, mods_s, caches, wts, n1, kvg, n2, fg,
                          batch=bs, seq=ss, past=past, tm=1024, tq=ss, tk=1152)
        states_p.append(st_p)
        states_s.append(st_s)

    def stack(states, i, batch, seq, tail):
        return jnp.stack([st[i].reshape((batch, seq) + tail) for st in states])

    tails = [(KV_LORA,), (QK_ROPE,), (HEADS, HEAD_DIM), (HEADS, HEAD_DIM), (HEADS,)]
    outs_p = [stack(states_p, i, bp, sp, tails[i]) for i in range(5)]
    outs_s = [stack(states_s, i, bs, ss, tails[i]) for i in range(5)]
    return (xp.reshape(bp, sp, d), xs.reshape(bs, ss, d), *outs_p, *outs_s)
```
